```python
import math
import jax, jax.numpy as jnp
from jax import lax
import numpy as np

D_MODEL = 1024
BATCH = 4
SEQ = 4096
DEPTH = 1

DA_HEADS = 4
DA_QK_DIM = 64
DA_V_DIM = 2 * DA_QK_DIM
DA_WIDTH = DA_HEADS * DA_V_DIM
DA_QK_COLS = DA_HEADS * 2 * DA_QK_DIM
DA_QBLOCK = 128
ML_HEADS = 4
ML_HEAD_DIM = 128
ML_WIDTH = ML_HEADS * ML_HEAD_DIM
ML_CHUNK = 64
ML_CONV = 4
MIX_WIDTH = DA_WIDTH + ML_WIDTH
IN_SIZES = (DA_QK_COLS, DA_QK_COLS, DA_WIDTH, ML_WIDTH, ML_WIDTH, ML_WIDTH, ML_WIDTH, ML_HEADS, ML_HEADS)
IN_COLS = DA_QK_COLS * 2 + DA_WIDTH + ML_WIDTH * 4 + ML_HEADS * 2
PEER_HEADS = 8
PEER_NKEYS = 128
PEER_EXPERTS = PEER_NKEYS * PEER_NKEYS
PEER_KEY_DIM = 256
PEER_HALF = PEER_KEY_DIM // 2
PEER_TOPK = 16
PEER_BLOCK = 128

NORM_EPS = 1e-6
SUBLN_EPS = 1e-5
NEG_INF = -1e30

kernel_name = 'hybrid_diffattn_mlstm_peer'


def _rmsnorm(x, g, eps=NORM_EPS):
    xf = x.astype(jnp.float32)
    y = xf * lax.rsqrt(jnp.mean(xf * xf, axis=-1, keepdims=True) + eps)
    return (y * g.astype(jnp.float32)).astype(x.dtype)


def _alibi_slopes(n):
    start = 2.0 ** (-8.0 / n)
    return jnp.asarray(np.array([start ** (h + 1) for h in range(n)], dtype=np.float32))


def _causal_dwconv(u, w, b):
    c = u.shape[-1]
    y = lax.conv_general_dilated(u, w[:, None, :].astype(u.dtype), window_strides=(1,),
                                 padding=[(w.shape[0] - 1, 0)],
                                 dimension_numbers=('NWC', 'WIO', 'NWC'),
                                 feature_group_count=c)
    return y + b.astype(u.dtype)


def _diff_attention(q, k, v, lam):
    bsz, seq = q.shape[0], q.shape[1]
    nb = seq // DA_QBLOCK
    scale = DA_QK_DIM ** -0.5
    qh = q.transpose(0, 2, 3, 1, 4)
    kh = k.transpose(0, 2, 3, 1, 4)
    vh = v.transpose(0, 2, 1, 3)
    qb = qh.reshape(bsz, DA_HEADS, 2, nb, DA_QBLOCK, DA_QK_DIM).transpose(3, 0, 1, 2, 4, 5)
    slopes = _alibi_slopes(DA_HEADS)
    kpos = jnp.arange(seq, dtype=jnp.int32)

    def block(args):
        qblk, start = args
        dist = (start + jnp.arange(DA_QBLOCK, dtype=jnp.int32))[:, None] - kpos[None, :]
        s = jnp.einsum('bhcqd,bhckd->bhcqk', qblk, kh).astype(jnp.float32) * scale
        s = s - (slopes[:, None, None] * dist.astype(jnp.float32))[None, :, None]
        s = jnp.where(dist >= 0, s, NEG_INF)
        p = jax.nn.softmax(s, axis=-1)
        a = p[:, :, 0] - lam * p[:, :, 1]
        return jnp.einsum('bhqk,bhkd->bhqd', a.astype(vh.dtype), vh)

    out = lax.map(block, (qb, jnp.arange(nb, dtype=jnp.int32) * DA_QBLOCK))
    return out.transpose(1, 0, 3, 2, 4).reshape(bsz, seq, DA_HEADS, DA_V_DIM)


def _mlstm(q, k, v, i_pre, f_pre):
    f32 = jnp.float32
    bsz, seq = q.shape[0], q.shape[1]
    L = ML_CHUNK
    nc = seq // L

    def chunk_vec(t):
        return t.astype(f32).reshape(bsz, nc, L, ML_HEADS, ML_HEAD_DIM).transpose(1, 0, 3, 2, 4)

    def chunk_gate(t):
        return t.astype(f32).reshape(bsz, nc, L, ML_HEADS).transpose(1, 0, 3, 2)

    qc = chunk_vec(q)
    kc = chunk_vec(k) * (ML_HEAD_DIM ** -0.5)
    vc = chunk_vec(v)
    ic = chunk_gate(i_pre)
    fc = chunk_gate(jax.nn.log_sigmoid(f_pre.astype(f32)))
    tri = jnp.tril(jnp.ones((L, L), dtype=bool))

    def step(carry, xs):
        C, n, m = carry
        qt, kt, vt, it, ft = xs
        b = jnp.cumsum(ft, axis=-1)
        D = jnp.where(tri, b[..., :, None] - b[..., None, :] + it[..., None, :], NEG_INF)
        inter = b + m[..., None]
        mt = jnp.maximum(inter, jnp.max(D, axis=-1))
        w_intra = jnp.exp(D - mt[..., None])
        w_inter = jnp.exp(inter - mt)
        s = jnp.einsum('bhtd,bhsd->bhts', qt, kt) * w_intra
        num = w_inter[..., None] * jnp.einsum('bhtd,bhdv->bhtv', qt, C) + jnp.einsum('bhts,bhsv->bhtv', s, vt)
        den = w_inter * jnp.einsum('bhtd,bhd->bht', qt, n) + jnp.sum(s, axis=-1)
        h = num / jnp.maximum(jnp.abs(den), jnp.exp(-mt))[..., None]
        m_new = mt[..., -1]
        ws = jnp.exp(b[..., -1:] - b + it - m_new[..., None])
        decay = jnp.exp(b[..., -1] + m - m_new)
        C = decay[..., None, None] * C + jnp.einsum('bhs,bhsd,bhsv->bhdv', ws, kt, vt)
        n = decay[..., None] * n + jnp.einsum('bhs,bhsd->bhd', ws, kt)
        return (C, n, m_new), h

    c0 = jnp.zeros((bsz, ML_HEADS, ML_HEAD_DIM, ML_HEAD_DIM), f32)
    n0 = jnp.zeros((bsz, ML_HEADS, ML_HEAD_DIM), f32)
    m0 = jnp.full((bsz, ML_HEADS), NEG_INF, f32)
    _, h = lax.scan(step, (c0, n0, m0), (qc, kc, vc, ic, fc))
    return h.transpose(1, 0, 3, 2, 4).reshape(bsz, seq, ML_HEADS, ML_HEAD_DIM)


def _peer(xn, w_pq, sub_keys, u_emb, v_emb):
    bsz, seq, d = xn.shape
    t = bsz * seq
    xt = xn.reshape(t, d)
    q = (xt @ w_pq).reshape(t, PEER_HEADS, 2, PEER_HALF)
    scores = jnp.einsum('thcd,hcnd->thcn', q, sub_keys).astype(jnp.float32)
    s1, i1 = lax.top_k(scores[:, :, 0], PEER_TOPK)
    s2, i2 = lax.top_k(scores[:, :, 1], PEER_TOPK)
    kk = PEER_TOPK * PEER_TOPK
    cand_s = (s1[..., :, None] + s2[..., None, :]).reshape(t, PEER_HEADS, kk)
    cand_i = (i1[..., :, None] * PEER_NKEYS + i2[..., None, :]).reshape(t, PEER_HEADS, kk)
    top_s, pos = lax.top_k(cand_s, PEER_TOPK)
    eidx = jnp.take_along_axis(cand_i, pos, axis=-1)
    gate = jax.nn.softmax(top_s, axis=-1).astype(xn.dtype)
    nb = t // PEER_BLOCK

    def block(args):
        xb, eb, gb = args
        act = jax.nn.gelu(jnp.einsum('thkd,td->thk', u_emb[eb], xb), approximate=False)
        return jnp.einsum('thk,thkd->td', gb * act, v_emb[eb])

    out = lax.map(block, (xt.reshape(nb, PEER_BLOCK, d),
                          eidx.reshape(nb, PEER_BLOCK, PEER_HEADS, PEER_TOPK),
                          gate.reshape(nb, PEER_BLOCK, PEER_HEADS, PEER_TOPK)))
    return out.reshape(bsz, seq, d)


def setup_inputs(seed: int = 0) -> dict:
    key = jax.random.key(seed)
    ks = jax.random.split(key, 20)
    nrm = jax.random.normal
    f32 = jnp.float32
    x = nrm(ks[0], (BATCH, SEQ, D_MODEL), f32)
    norm1_g = 1.0 + 0.01 * nrm(ks[1], (DEPTH, D_MODEL), f32)
    w_in = nrm(ks[2], (DEPTH, D_MODEL, IN_COLS), f32) * D_MODEL ** -0.5
    conv_w = nrm(ks[3], (DEPTH, ML_CONV, 2 * ML_WIDTH), f32) * ML_CONV ** -0.5
    conv_b = 0.01 * nrm(ks[4], (DEPTH, 2 * ML_WIDTH), f32)
    b_igate = 0.1 * nrm(ks[5], (DEPTH, ML_HEADS), f32)
    b_fgate = jnp.linspace(3.0, 6.0, ML_HEADS, dtype=f32)[None, :] + 0.01 * nrm(ks[6], (DEPTH, ML_HEADS), f32)
    lam_q1 = 0.1 * nrm(ks[7], (DEPTH, DA_QK_DIM), f32)
    lam_k1 = 0.1 * nrm(ks[8], (DEPTH, DA_QK_DIM), f32)
    lam_q2 = 0.1 * nrm(ks[9], (DEPTH, DA_QK_DIM), f32)
    lam_k2 = 0.1 * nrm(ks[10], (DEPTH, DA_QK_DIM), f32)
    da_subln_g = 1.0 + 0.01 * nrm(ks[11], (DEPTH, DA_V_DIM), f32)
    ml_norm_g = 1.0 + 0.01 * nrm(ks[12], (DEPTH, ML_WIDTH), f32)
    w_out = nrm(ks[13], (DEPTH, MIX_WIDTH, D_MODEL), f32) * MIX_WIDTH ** -0.5
    norm2_g = 1.0 + 0.01 * nrm(ks[14], (DEPTH, D_MODEL), f32)
    w_pq = nrm(ks[15], (DEPTH, D_MODEL, PEER_HEADS * PEER_KEY_DIM), f32) * D_MODEL ** -0.5
    sub_keys = nrm(ks[16], (DEPTH, PEER_HEADS, 2, PEER_NKEYS, PEER_HALF), f32) * PEER_HALF ** -0.5
    u_emb = nrm(ks[17], (DEPTH, PEER_EXPERTS, D_MODEL), f32) * D_MODEL ** -0.5
    v_emb = nrm(ks[18], (DEPTH, PEER_EXPERTS, D_MODEL), f32) * PEER_HEADS ** -0.5
    final_g = 1.0 + 0.01 * nrm(ks[19], (D_MODEL,), f32)
    return {'x': x, 'norm1_g': norm1_g, 'w_in': w_in, 'conv_w': conv_w, 'conv_b': conv_b,
            'b_igate': b_igate, 'b_fgate': b_fgate, 'lam_q1': lam_q1, 'lam_k1': lam_k1,
            'lam_q2': lam_q2, 'lam_k2': lam_k2, 'da_subln_g': da_subln_g, 'ml_norm_g': ml_norm_g,
            'w_out': w_out, 'norm2_g': norm2_g, 'w_pq': w_pq, 'sub_keys': sub_keys,
            'u_emb': u_emb, 'v_emb': v_emb, 'final_g': final_g}


def reference(x, norm1_g, w_in, conv_w, conv_b, b_igate, b_fgate, lam_q1, lam_k1, lam_q2, lam_k2,
              da_subln_g, ml_norm_g, w_out, norm2_g, w_pq, sub_keys, u_emb, v_emb, final_g):
    bsz, seq = x.shape[0], x.shape[1]
    split_at = np.cumsum(np.array(IN_SIZES))[:-1].tolist()
    for l in range(DEPTH):
        lam_init = 0.8 - 0.6 * math.exp(-0.3 * l)
        h = _rmsnorm(x, norm1_g[l])
        proj = h @ w_in[l]
        da_q, da_k, da_v, ml_q, ml_k, ml_v, ml_o, ml_i, ml_f = jnp.split(proj, split_at, axis=-1)
        lam = (jnp.exp(jnp.sum(lam_q1[l] * lam_k1[l]).astype(jnp.float32))
               - jnp.exp(jnp.sum(lam_q2[l] * lam_k2[l]).astype(jnp.float32)) + lam_init)
        a = _diff_attention(da_q.reshape(bsz, seq, DA_HEADS, 2, DA_QK_DIM),
                            da_k.reshape(bsz, seq, DA_HEADS, 2, DA_QK_DIM),
                            da_v.reshape(bsz, seq, DA_HEADS, DA_V_DIM), lam)
        a = (_rmsnorm(a, da_subln_g[l], SUBLN_EPS) * (1.0 - lam_init)).reshape(bsz, seq, DA_WIDTH)
        qk = jax.nn.silu(_causal_dwconv(jnp.concatenate([ml_q, ml_k], axis=-1), conv_w[l], conv_b[l]))
        mq, mk = jnp.split(qk, 2, axis=-1)
        hm = _mlstm(mq.reshape(bsz, seq, ML_HEADS, ML_HEAD_DIM),
                    mk.reshape(bsz, seq, ML_HEADS, ML_HEAD_DIM),
                    ml_v.reshape(bsz, seq, ML_HEADS, ML_HEAD_DIM),
                    ml_i + b_igate[l], ml_f + b_fgate[l]).astype(x.dtype)
        hm = _rmsnorm(hm, ml_norm_g[l].reshape(ML_HEADS, ML_HEAD_DIM))
        hm = (hm * jax.nn.sigmoid(ml_o).reshape(bsz, seq, ML_HEADS, ML_HEAD_DIM)).reshape(bsz, seq, ML_WIDTH)
        x = x + jnp.concatenate([a, hm], axis=-1) @ w_out[l]
        x = x + _peer(_rmsnorm(x, norm2_g[l]), w_pq[l], sub_keys[l], u_emb[l], v_emb[l])
    return _rmsnorm(x, final_g)
```

```python
import functools

import numpy as np
import jax
import jax.numpy as jnp
from jax import lax
from jax.experimental import pallas as pl
from jax.experimental.pallas import tpu as pltpu

F32 = jnp.float32
BF16 = jnp.bfloat16

D_MODEL = 1024
DA_HEADS = 4
DA_QK_DIM = 64
DA_V_DIM = 128
DA_WIDTH = 512
ML_HEADS = 4
ML_HEAD_DIM = 128
ML_WIDTH = 512
ML_CONV = 4
N_MAIN = 7 * 512
PEER_HEADS = 8
PEER_NKEYS = 128
PEER_EXPERTS = PEER_NKEYS * PEER_NKEYS
PEER_HALF = 128
PEER_TOPK = 16
NORM_EPS = 1e-6
SUBLN_EPS = 1e-5
NEG_INF = -1e30
LAM_INIT = 0.2
LANES = 128
SUBLANES = 8

IN_TOKENS = 512
DA_TQ = 256
DA_TK = 256
ML_CHUNK = 256
MID_TOKENS = 256
PEER_TOKENS = 512
PEER_ECHUNK = 1024
N_EXTRACT = PEER_TOPK + 1
CAND_COUNTS = tuple(N_EXTRACT // (i + 1) for i in range(N_EXTRACT))
N_CAND = sum(CAND_COUNTS)
N_CAND_PAD = -(-N_CAND // SUBLANES) * SUBLANES
VMEM_LIMIT = 56 * 1024 * 1024


def _nt_dot(a, b):
    return lax.dot_general(a, b, (((1,), (1,)), ((), ())), preferred_element_type=F32)


def _inproj_kernel(x_ref, g_ref, w_ref, wg_ref, bg_ref,
                   q0_ref, q1_ref, dak_ref, dav_ref, mlq_ref, mlk_ref, mlv_ref, mlo_ref, gate_ref):
    x = x_ref[...]
    h = x * lax.rsqrt(jnp.mean(x * x, axis=-1, keepdims=True) + NORM_EPS) * g_ref[...]
    hb = h.astype(BF16)

    def mm(group):
        return jnp.dot(hb, w_ref[:, group * 512:(group + 1) * 512], preferred_element_type=F32)

    q = mm(0) * (DA_QK_DIM ** -0.5)
    lane = lax.broadcasted_iota(jnp.int32, q.shape, 1)
    first = (lane % (2 * DA_QK_DIM)) < DA_QK_DIM
    q0_ref[...] = jnp.where(first, q, 0.0).astype(BF16)
    q1_ref[...] = jnp.where(first, 0.0, q).astype(BF16)
    dak_ref[...] = mm(1).astype(BF16)
    dav_ref[...] = mm(2).astype(BF16)
    mlq_ref[...] = mm(3)
    mlk_ref[...] = mm(4)
    mlv_ref[...] = mm(5).astype(BF16)
    mlo_ref[...] = mm(6)
    gate_ref[...] = jnp.dot(hb, wg_ref[...], preferred_element_type=F32) + bg_ref[...]


def _inproj(x2, norm1_g, w_main, w_gate, b_gate):
    t = x2.shape[0]
    tt = min(IN_TOKENS, t)
    row = lambda i: (i, 0)
    fixed = lambda i: (0, 0)
    wide = lambda dt: jax.ShapeDtypeStruct((t, 512), dt)
    return pl.pallas_call(
        _inproj_kernel,
        grid=(t // tt,),
        in_specs=[pl.BlockSpec((tt, D_MODEL), row),
                  pl.BlockSpec((1, D_MODEL), fixed),
                  pl.BlockSpec((D_MODEL, N_MAIN), fixed),
                  pl.BlockSpec((D_MODEL, 2 * LANES), fixed),
                  pl.BlockSpec((1, 2 * LANES), fixed)],
        out_specs=[pl.BlockSpec((tt, 512), row)] * 8 + [pl.BlockSpec((tt, 2 * LANES), row)],
        out_shape=[wide(BF16), wide(BF16), wide(BF16), wide(BF16), wide(F32), wide(F32), wide(BF16), wide(F32),
                   jax.ShapeDtypeStruct((t, 2 * LANES), F32)],
        compiler_params=pltpu.CompilerParams(dimension_semantics=("arbitrary",), vmem_limit_bytes=VMEM_LIMIT),
        name="inproj",
    )(x2, norm1_g, w_main, w_gate, b_gate)


def _da_kernel(slope_ref, lamv_ref, g_ref, q0_ref, q1_ref, k_ref, v_ref, o_ref, m_s, l_s, acc_s):
    h = pl.program_id(1)
    i = pl.program_id(2)
    tq = q0_ref.shape[0]
    slope = slope_ref[h]
    q2 = jnp.concatenate([q0_ref[...], q1_ref[...]], axis=0)
    m_s[...] = jnp.full(m_s.shape, NEG_INF, F32)
    l_s[...] = jnp.zeros(l_s.shape, F32)
    acc_s[...] = jnp.zeros(acc_s.shape, F32)

    def step(j, masked):
        kstart = pl.multiple_of(j * DA_TK, DA_TK)
        kj = k_ref[pl.ds(kstart, DA_TK), :]
        vj = v_ref[pl.ds(kstart, DA_TK), :]
        s = _nt_dot(q2, kj)
        krel = (kstart - i * tq + lax.broadcasted_iota(jnp.int32, (1, DA_TK), 1)).astype(F32)
        s = s + slope * krel
        if masked:
            row = lax.broadcasted_iota(jnp.int32, s.shape, 0)
            col = lax.broadcasted_iota(jnp.int32, s.shape, 1)
            rq = jnp.where(row >= tq, row - tq, row)
            s = jnp.where(col <= rq, s, NEG_INF)
        m_prev = m_s[...]
        m_new = jnp.maximum(m_prev, jnp.max(s, axis=-1, keepdims=True))
        alpha = jnp.exp(m_prev - m_new)
        p = jnp.exp(s - m_new)
        l_s[...] = alpha * l_s[...] + jnp.sum(p, axis=-1, keepdims=True)
        acc_s[...] = alpha * acc_s[...] + jnp.dot(p.astype(BF16), vj, preferred_element_type=F32)
        m_s[...] = m_new

    def body(j, carry):
        step(j, False)
        return carry

    lax.fori_loop(0, i, body, 0)
    step(i, True)

    o = acc_s[...] / l_s[...]
    lamv = lamv_ref[...]
    lam = (jnp.exp(jnp.sum(lamv[0:1] * lamv[1:2], axis=-1, keepdims=True))
           - jnp.exp(jnp.sum(lamv[2:3] * lamv[3:4], axis=-1, keepdims=True)) + LAM_INIT)
    a = o[:tq] - lam * o[tq:]
    y = a * lax.rsqrt(jnp.mean(a * a, axis=-1, keepdims=True) + SUBLN_EPS) * g_ref[...]
    o_ref[...] = (y * (1.0 - LAM_INIT)).astype(BF16)


def _diff_attention(q0, q1, k, v, slopes, lamv, subln_g):
    b, s, _ = q0.shape
    assert DA_TQ == DA_TK and s % DA_TQ == 0
    qspec = pl.BlockSpec((None, DA_TQ, LANES), lambda bi, hi, qi: (bi, qi, hi))
    kvspec = pl.BlockSpec((None, s, LANES), lambda bi, hi, qi: (bi, 0, hi))
    return pl.pallas_call(
        _da_kernel,
        grid=(b, DA_HEADS, s // DA_TQ),
        in_specs=[pl.BlockSpec(memory_space=pltpu.SMEM),
                  pl.BlockSpec((4, DA_QK_DIM), lambda bi, hi, qi: (0, 0)),
                  pl.BlockSpec((1, DA_V_DIM), lambda bi, hi, qi: (0, 0)),
                  qspec, qspec, kvspec, kvspec],
        out_specs=qspec,
        out_shape=jax.ShapeDtypeStruct((b, s, DA_WIDTH), BF16),
        scratch_shapes=[pltpu.VMEM((2 * DA_TQ, 1), F32), pltpu.VMEM((2 * DA_TQ, 1), F32),
                        pltpu.VMEM((2 * DA_TQ, DA_V_DIM), F32)],
        compiler_params=pltpu.CompilerParams(dimension_semantics=("arbitrary",) * 3, vmem_limit_bytes=VMEM_LIMIT),
        name="diff_attention",
    )(slopes, lamv, subln_g, q0, q1, k, v)


def _mlstm_kernel(tril_ref, cw_ref, cb_ref, ng_ref, q_ref, k_ref, v_ref, og_ref, gate_ref, out_ref,
                  xbuf, cn_s, m_s):
    c = pl.program_id(1)
    L = q_ref.shape[0]

    @pl.when(c == 0)
    def _():
        xbuf[0:SUBLANES, :] = jnp.zeros((SUBLANES, 2 * ML_WIDTH), F32)
        cn_s[...] = jnp.zeros(cn_s.shape, F32)
        m_s[...] = jnp.full(m_s.shape, NEG_INF, F32)

    @pl.when(c > 0)
    def _():
        xbuf[0:SUBLANES, :] = xbuf[L:L + SUBLANES, :]

    xbuf[SUBLANES:SUBLANES + L, 0:ML_WIDTH] = q_ref[...]
    xbuf[SUBLANES:SUBLANES + L, ML_WIDTH:2 * ML_WIDTH] = k_ref[...]
    y = cb_ref[...] + cw_ref[0:1, :] * xbuf[SUBLANES - 3:SUBLANES - 3 + L, :]
    for j in range(1, ML_CONV):
        y = y + cw_ref[j:j + 1, :] * xbuf[SUBLANES - 3 + j:SUBLANES - 3 + j + L, :]
    qk = y * jax.nn.sigmoid(y)

    gates = gate_ref[...]
    i_log = gates[:, 0:LANES]
    f_log = jax.nn.log_sigmoid(gates[:, LANES:2 * LANES])
    b_all = jnp.dot(tril_ref[...], f_log, preferred_element_type=F32,
                    precision=lax.Precision.HIGHEST)
    r_all = i_log - b_all
    r_all_t = r_all.T
    row = lax.broadcasted_iota(jnp.int32, (L, L), 0)
    col = lax.broadcasted_iota(jnp.int32, (L, L), 1)
    tri = col <= row
    one_col = jnp.where(lax.broadcasted_iota(jnp.int32, (L, LANES), 1) == 0, 1.0, 0.0).astype(BF16)

    for h in range(ML_HEADS):
        hs = slice(h * ML_HEAD_DIM, (h + 1) * ML_HEAD_DIM)
        b_col = b_all[:, h:h + 1]
        r_col = r_all[:, h:h + 1]
        r_row = r_all_t[h:h + 1, :]
        m_prev = m_s[h:h + 1, 0:1]
        d = jnp.where(tri, b_col + r_row, NEG_INF)
        inter = b_col + m_prev
        mt = jnp.maximum(inter, jnp.max(d, axis=-1, keepdims=True))
        w_intra = jnp.exp(d - mt)
        w_inter = jnp.exp(inter - mt)
        qb = qk[:, hs].astype(BF16)
        kf = qk[:, ML_WIDTH + h * ML_HEAD_DIM:ML_WIDTH + (h + 1) * ML_HEAD_DIM] * (ML_HEAD_DIM ** -0.5)
        s = _nt_dot(qb, kf.astype(BF16)) * w_intra
        v_ext = jnp.concatenate([v_ref[:, hs], one_col], axis=1)
        cn = cn_s[h]
        num_ext = (w_inter * jnp.dot(qb, cn.astype(BF16), preferred_element_type=F32)
                   + jnp.dot(s.astype(BF16), v_ext, preferred_element_type=F32))
        num = num_ext[:, 0:ML_HEAD_DIM]
        den = num_ext[:, ML_HEAD_DIM:ML_HEAD_DIM + 1]
        hh = num / jnp.maximum(jnp.abs(den), jnp.exp(-mt))
        m_new = mt[L - 1:L, :]
        b_last = b_col[L - 1:L, :]
        ws = jnp.exp(b_last + r_col - m_new)
        decay = jnp.exp(b_last + m_prev - m_new)
        kw_t = (kf * ws).T.astype(BF16)
        cn_s[h] = decay * cn + jnp.dot(kw_t, v_ext, preferred_element_type=F32)
        m_s[h:h + 1, :] = jnp.broadcast_to(m_new, (1, LANES))
        hn = hh * lax.rsqrt(jnp.mean(hh * hh, axis=-1, keepdims=True) + NORM_EPS) * ng_ref[:, hs]
        out_ref[:, hs] = (hn * jax.nn.sigmoid(og_ref[:, hs])).astype(BF16)


def _mlstm(ml_q, ml_k, ml_v, ml_o, gates, conv_w, conv_b, norm_g):
    b, s, _ = ml_q.shape
    L = min(ML_CHUNK, s)
    tril = jnp.tril(jnp.ones((L, L), F32))
    fixed = lambda bi, ci: (0, 0)
    seq = lambda w: pl.BlockSpec((None, L, w), lambda bi, ci: (bi, ci, 0))
    return pl.pallas_call(
        _mlstm_kernel,
        grid=(b, s // L),
        in_specs=[pl.BlockSpec((L, L), fixed),
                  pl.BlockSpec((ML_CONV, 2 * ML_WIDTH), fixed),
                  pl.BlockSpec((1, 2 * ML_WIDTH), fixed),
                  pl.BlockSpec((1, ML_WIDTH), fixed),
                  seq(ML_WIDTH), seq(ML_WIDTH), seq(ML_WIDTH), seq(ML_WIDTH), seq(2 * LANES)],
        out_specs=seq(ML_WIDTH),
        out_shape=jax.ShapeDtypeStruct((b, s, ML_WIDTH), BF16),
        scratch_shapes=[pltpu.VMEM((L + 2 * SUBLANES, 2 * ML_WIDTH), F32),
                        pltpu.VMEM((ML_HEADS, ML_HEAD_DIM, 2 * ML_HEAD_DIM), F32),
                        pltpu.VMEM((SUBLANES, LANES), F32)],
        compiler_params=pltpu.CompilerParams(dimension_semantics=("arbitrary",) * 2, vmem_limit_bytes=VMEM_LIMIT),
        name="mlstm",
    )(tril, conv_w, conv_b, norm_g, ml_q, ml_k, ml_v, ml_o, gates)


def _extract_top(s, n, out_ref, lead):
    rows = s.shape[0]
    iota = lax.broadcasted_iota(jnp.int32, s.shape, 0)
    for k in range(n):
        mx = jnp.max(s, axis=0, keepdims=True)
        out_ref[lead, k:k + 1, :] = mx
        if k + 1 < n:
            first = jnp.min(jnp.where(s == mx, iota, rows), axis=0, keepdims=True)
            s = jnp.where(iota == first, -jnp.inf, s)


def _mid_kernel(a_ref, hm_ref, x_ref, wout_ref, g2_ref, wpq_ref, sk_ref,
                x1_ref, xnt_ref, st_ref, meta_ref, q_s, vals_s, cand_s):
    x1 = (x_ref[...]
          + jnp.dot(a_ref[...], wout_ref[0:DA_WIDTH, :], preferred_element_type=F32)
          + jnp.dot(hm_ref[...], wout_ref[DA_WIDTH:DA_WIDTH + ML_WIDTH, :], preferred_element_type=F32))
    x1_ref[...] = x1
    xn = x1 * lax.rsqrt(jnp.mean(x1 * x1, axis=-1, keepdims=True) + NORM_EPS) * g2_ref[...]
    xnt_ref[...] = xn.T.astype(BF16)
    q = jnp.dot(xn.astype(BF16), wpq_ref[...], preferred_element_type=F32).astype(BF16)
    for hc in range(2 * PEER_HEADS):
        q_s[hc] = q[:, hc * PEER_HALF:(hc + 1) * PEER_HALF]
    tt = x1.shape[0]
    cand_s[N_CAND:N_CAND_PAD, :] = jnp.full((N_CAND_PAD - N_CAND, tt), -jnp.inf, F32)

    def head(h, carry):
        for c in range(2):
            hc = 2 * h + c
            s = _nt_dot(sk_ref[hc], q_s[hc])
            st_ref[hc] = s
            _extract_top(s, N_EXTRACT, vals_s, c)
        off = 0
        for i, cnt in enumerate(CAND_COUNTS):
            cand_s[off:off + cnt, :] = vals_s[0, i:i + 1, :] + vals_s[1, 0:cnt, :]
            off += cnt
        _extract_top(cand_s[...], N_EXTRACT, vals_s, 2)
        top = vals_s[2, 0:PEER_TOPK, :]
        t0 = top[0:1, :]
        lse = t0 + jnp.log(jnp.sum(jnp.exp(top - t0), axis=0, keepdims=True))
        thr = top[PEER_TOPK - 1:PEER_TOPK, :]
        c1 = vals_s[0, 0:1, :]
        tie = jnp.where(vals_s[2, PEER_TOPK:PEER_TOPK + 1, :] == thr, 1.0, 0.0)
        meta_ref[h, 0:1, :] = thr
        meta_ref[h, 1:2, :] = c1
        meta_ref[h, 2:3, :] = lse - c1
        meta_ref[h, 3:4, :] = tie
        meta_ref[h, 4:SUBLANES, :] = jnp.zeros((SUBLANES - 4, tt), F32)
        return carry

    lax.fori_loop(0, PEER_HEADS, head, 0)


def _mid(a, hm, x2, w_out, norm2_g, w_pq, sub_keys):
    t = x2.shape[0]
    tt = min(MID_TOKENS, t)
    row = lambda i: (i, 0)
    fixed = lambda i: (0, 0)
    return pl.pallas_call(
        _mid_kernel,
        grid=(t // tt,),
        in_specs=[pl.BlockSpec((tt, DA_WIDTH), row),
                  pl.BlockSpec((tt, ML_WIDTH), row),
                  pl.BlockSpec((tt, D_MODEL), row),
                  pl.BlockSpec((D_MODEL, D_MODEL), fixed),
                  pl.BlockSpec((1, D_MODEL), fixed),
                  pl.BlockSpec((D_MODEL, 2 * PEER_HEADS * PEER_HALF), fixed),
                  pl.BlockSpec((2 * PEER_HEADS, PEER_NKEYS, PEER_HALF), lambda i: (0, 0, 0))],
        out_specs=[pl.BlockSpec((tt, D_MODEL), row),
                   pl.BlockSpec((D_MODEL, tt), lambda i: (0, i)),
                   pl.BlockSpec((2 * PEER_HEADS, PEER_NKEYS, tt), lambda i: (0, 0, i)),
                   pl.BlockSpec((PEER_HEADS, SUBLANES, tt), lambda i: (0, 0, i))],
        out_shape=[jax.ShapeDtypeStruct((t, D_MODEL), F32),
                   jax.ShapeDtypeStruct((D_MODEL, t), BF16),
                   jax.ShapeDtypeStruct((2 * PEER_HEADS, PEER_NKEYS, t), F32),
                   jax.ShapeDtypeStruct((PEER_HEADS, SUBLANES, t), F32)],
        scratch_shapes=[pltpu.VMEM((2 * PEER_HEADS, tt, PEER_HALF), BF16),
                        pltpu.VMEM((3, 3 * SUBLANES, tt), F32),
                        pltpu.VMEM((N_CAND_PAD, tt), F32)],
        compiler_params=pltpu.CompilerParams(dimension_semantics=("arbitrary",), vmem_limit_bytes=VMEM_LIMIT),
        name="mid",
    )(a, hm, x2, w_out, norm2_g, w_pq, sub_keys)


def _gelu(x):
    return 0.5 * x * (1.0 + lax.erf(x * np.float32(np.sqrt(0.5))))


def _peer_kernel(xnt_ref, u_ref, vt_ref, st_ref, meta_ref, x1_ref, fg_ref, y_ref, e1_s, e2_s, g_s, acc_s):
    e = pl.program_id(1)
    tt = xnt_ref.shape[1]

    @pl.when(e == 0)
    def _():
        for h in range(PEER_HEADS):
            e1_s[h] = jnp.exp(st_ref[2 * h] - meta_ref[h, 1:2, :])
            e2_s[h] = jnp.exp(st_ref[2 * h + 1] - meta_ref[h, 2:3, :])
        acc_s[...] = jnp.zeros(acc_s.shape, F32)

    act_t = jnp.dot(u_ref[...], xnt_ref[...], preferred_element_type=F32)
    for j in range(PEER_ECHUNK // PEER_NKEYS):
        a_idx = e * (PEER_ECHUNK // PEER_NKEYS) + j
        w = None
        for h in range(PEER_HEADS):
            s1 = st_ref[2 * h, pl.ds(a_idx, 1), :]
            e1 = e1_s[h, pl.ds(a_idx, 1), :]
            sel = (s1 + st_ref[2 * h + 1]) >= meta_ref[h, 0:1, :]
            wh = jnp.where(sel, e1 * e2_s[h], 0.0)
            w = wh if w is None else w + wh
        g_s[j * PEER_NKEYS:(j + 1) * PEER_NKEYS, :] = (
            _gelu(act_t[j * PEER_NKEYS:(j + 1) * PEER_NKEYS, :]) * w).astype(BF16)
    acc_s[...] += jnp.dot(vt_ref[...], g_s[...], preferred_element_type=F32)

    @pl.when(e == pl.num_programs(1) - 1)
    def _():
        x2 = x1_ref[...] + acc_s[...].T
        y_ref[...] = x2 * lax.rsqrt(jnp.mean(x2 * x2, axis=-1, keepdims=True) + NORM_EPS) * fg_ref[...]


def _peer(xnt, u_bf, vt_bf, st, meta, x1, final_g):
    t = x1.shape[0]
    tt = min(PEER_TOKENS, t)
    return pl.pallas_call(
        _peer_kernel,
        grid=(t // tt, PEER_EXPERTS // PEER_ECHUNK),
        in_specs=[pl.BlockSpec((D_MODEL, tt), lambda i, e: (0, i)),
                  pl.BlockSpec((PEER_ECHUNK, D_MODEL), lambda i, e: (e, 0)),
                  pl.BlockSpec((D_MODEL, PEER_ECHUNK), lambda i, e: (0, e)),
                  pl.BlockSpec((2 * PEER_HEADS, PEER_NKEYS, tt), lambda i, e: (0, 0, i)),
                  pl.BlockSpec((PEER_HEADS, SUBLANES, tt), lambda i, e: (0, 0, i)),
                  pl.BlockSpec((tt, D_MODEL), lambda i, e: (i, 0)),
                  pl.BlockSpec((1, D_MODEL), lambda i, e: (0, 0))],
        out_specs=pl.BlockSpec((tt, D_MODEL), lambda i, e: (i, 0)),
        out_shape=jax.ShapeDtypeStruct((t, D_MODEL), F32),
        scratch_shapes=[pltpu.VMEM((PEER_HEADS, PEER_NKEYS, tt), F32),
                        pltpu.VMEM((PEER_HEADS, PEER_NKEYS, tt), F32),
                        pltpu.VMEM((PEER_ECHUNK, tt), BF16),
                        pltpu.VMEM((D_MODEL, tt), F32)],
        compiler_params=pltpu.CompilerParams(dimension_semantics=("arbitrary",) * 2, vmem_limit_bytes=VMEM_LIMIT),
        name="peer_experts",
    )(xnt, u_bf, vt_bf, st, meta, x1, final_g)


def _alibi_slopes(n):
    start = 2.0 ** (-8.0 / n)
    return jnp.asarray(np.array([start ** (h + 1) for h in range(n)], dtype=np.float32))


def kernel(x, norm1_g, w_in, conv_w, conv_b, b_igate, b_fgate, lam_q1, lam_k1, lam_q2, lam_k2, da_subln_g, ml_norm_g, w_out, norm2_g, w_pq, sub_keys, u_emb, v_emb, final_g):
    bsz, seq, d = x.shape
    t = bsz * seq
    x2 = x.reshape(t, d)
    l = 0
    w_main = w_in[l][:, :N_MAIN].astype(BF16)
    gate_pad = jnp.zeros((d, LANES - ML_HEADS), F32)
    w_gate = jnp.concatenate([w_in[l][:, N_MAIN:N_MAIN + ML_HEADS], gate_pad,
                              w_in[l][:, N_MAIN + ML_HEADS:], gate_pad], axis=1).astype(BF16)
    bias_pad = jnp.zeros((LANES - ML_HEADS,), F32)
    b_gate = jnp.concatenate([b_igate[l], bias_pad, b_fgate[l], bias_pad])[None, :]
    lamv = jnp.stack([lam_q1[l], lam_k1[l], lam_q2[l], lam_k2[l]])

    q0, q1, da_k, da_v, ml_q, ml_k, ml_v, ml_o, gates = _inproj(x2, norm1_g[l][None, :], w_main, w_gate, b_gate)
    seq3 = lambda z: z.reshape(bsz, seq, z.shape[-1])
    a = _diff_attention(seq3(q0), seq3(q1), seq3(da_k), seq3(da_v), _alibi_slopes(DA_HEADS), lamv,
                        da_subln_g[l][None, :])
    hm = _mlstm(seq3(ml_q), seq3(ml_k), seq3(ml_v), seq3(ml_o), seq3(gates),
                conv_w[l], conv_b[l][None, :], ml_norm_g[l][None, :])
    x1, xnt, st, meta = _mid(a.reshape(t, DA_WIDTH), hm.reshape(t, ML_WIDTH), x2, w_out[l].astype(BF16),
                             norm2_g[l][None, :], w_pq[l].astype(BF16),
                             sub_keys[l].reshape(2 * PEER_HEADS, PEER_NKEYS, PEER_HALF).astype(BF16))
    y = _peer(xnt, u_emb[l].astype(BF16), v_emb[l].T.astype(BF16), st, meta, x1, final_g[None, :])
    return y.reshape(bsz, seq, d)
```

```python
import functools

import numpy as np
import jax
import jax.numpy as jnp
from jax import lax
from jax.experimental import pallas as pl
from jax.experimental.pallas import tpu as pltpu

F32 = jnp.float32
BF16 = jnp.bfloat16

D_MODEL = 1024
DA_HEADS = 4
DA_QK_DIM = 64
DA_V_DIM = 128
DA_WIDTH = 512
ML_HEADS = 4
ML_HEAD_DIM = 128
ML_WIDTH = 512
ML_CONV = 4
N_MAIN = 7 * 512
PEER_HEADS = 8
PEER_NKEYS = 128
PEER_EXPERTS = PEER_NKEYS * PEER_NKEYS
PEER_HALF = 128
PEER_TOPK = 16
NORM_EPS = 1e-6
SUBLN_EPS = 1e-5
NEG_INF = -1e30
LAM_INIT = 0.2
LANES = 128
SUBLANES = 8

IN_TOKENS = 512
DA_TQ = 256
DA_TK = 512
DA_HEADS_PER_STEP = 2
ML_CHUNK = 256
MID_TOKENS = 256
PEER_TOKENS = 512
PEER_ECHUNK = 1024
N_EXTRACT = PEER_TOPK + 1
CAND_COUNTS = tuple(N_EXTRACT // (i + 1) for i in range(N_EXTRACT))
N_CAND = sum(CAND_COUNTS)
N_CAND_PAD = -(-N_CAND // SUBLANES) * SUBLANES
VMEM_LIMIT = 56 * 1024 * 1024


def _nt_dot(a, b):
    return lax.dot_general(a, b, (((1,), (1,)), ((), ())), preferred_element_type=F32)


def _inproj_kernel(x_ref, g_ref, w_ref, wg_ref, bg_ref,
                   q0_ref, q1_ref, dak_ref, dav_ref, mlq_ref, mlk_ref, mlv_ref, mlo_ref, gate_ref):
    x = x_ref[...]
    h = x * lax.rsqrt(jnp.mean(x * x, axis=-1, keepdims=True) + NORM_EPS) * g_ref[...]
    hb = h.astype(BF16)

    def mm(group):
        return jnp.dot(hb, w_ref[:, group * 512:(group + 1) * 512], preferred_element_type=F32)

    q = mm(0) * (DA_QK_DIM ** -0.5)
    lane = lax.broadcasted_iota(jnp.int32, q.shape, 1)
    first = (lane % (2 * DA_QK_DIM)) < DA_QK_DIM
    q0_ref[...] = jnp.where(first, q, 0.0).astype(BF16)
    q1_ref[...] = jnp.where(first, 0.0, q).astype(BF16)
    dak_ref[...] = mm(1).astype(BF16)
    dav_ref[...] = mm(2).astype(BF16)
    mlq_ref[...] = mm(3)
    mlk_ref[...] = mm(4)
    mlv_ref[...] = mm(5).astype(BF16)
    mlo_ref[...] = mm(6)
    gate_ref[...] = jnp.dot(hb, wg_ref[...], preferred_element_type=F32) + bg_ref[...]


def _inproj(x2, norm1_g, w_main, w_gate, b_gate):
    t = x2.shape[0]
    tt = min(IN_TOKENS, t)
    row = lambda i: (i, 0)
    fixed = lambda i: (0, 0)
    wide = lambda dt: jax.ShapeDtypeStruct((t, 512), dt)
    return pl.pallas_call(
        _inproj_kernel,
        grid=(t // tt,),
        in_specs=[pl.BlockSpec((tt, D_MODEL), row),
                  pl.BlockSpec((1, D_MODEL), fixed),
                  pl.BlockSpec((D_MODEL, N_MAIN), fixed),
                  pl.BlockSpec((D_MODEL, 2 * LANES), fixed),
                  pl.BlockSpec((1, 2 * LANES), fixed)],
        out_specs=[pl.BlockSpec((tt, 512), row)] * 8 + [pl.BlockSpec((tt, 2 * LANES), row)],
        out_shape=[wide(BF16), wide(BF16), wide(BF16), wide(BF16), wide(F32), wide(F32), wide(BF16), wide(F32),
                   jax.ShapeDtypeStruct((t, 2 * LANES), F32)],
        compiler_params=pltpu.CompilerParams(dimension_semantics=("arbitrary",), vmem_limit_bytes=VMEM_LIMIT),
        name="inproj",
    )(x2, norm1_g, w_main, w_gate, b_gate)


def _da_kernel(slope_ref, lamv_ref, g_ref, q0_ref, q1_ref, k_ref, v_ref, o_ref, m_s, acc_s):
    hp = pl.program_id(1)
    i = pl.program_id(2)
    tq = q0_ref.shape[0]
    m_s[...] = jnp.full(m_s.shape, NEG_INF, F32)
    acc_s[...] = jnp.zeros(acc_s.shape, F32)
    one_col = jnp.where(lax.broadcasted_iota(jnp.int32, (DA_TK, LANES), 1) == 0, 1.0, 0.0).astype(BF16)

    def step(j, masked):
        kstart = pl.multiple_of(j * DA_TK, DA_TK)
        krel = (kstart - i * tq + lax.broadcasted_iota(jnp.int32, (1, DA_TK), 1)).astype(F32)
        for hh in range(DA_HEADS_PER_STEP):
            hs = slice(hh * LANES, (hh + 1) * LANES)
            q2 = jnp.concatenate([q0_ref[:, hs], q1_ref[:, hs]], axis=0)
            kj = k_ref[pl.ds(kstart, DA_TK), hs]
            v_ext = jnp.concatenate([v_ref[pl.ds(kstart, DA_TK), hs], one_col], axis=1)
            s = _nt_dot(q2, kj) + slope_ref[hp * DA_HEADS_PER_STEP + hh] * krel
            if masked:
                row = lax.broadcasted_iota(jnp.int32, s.shape, 0)
                col = lax.broadcasted_iota(jnp.int32, s.shape, 1)
                rq = jnp.where(row >= tq, row - tq, row)
                s = jnp.where(col + (kstart - i * tq) <= rq, s, NEG_INF)
            m_prev = m_s[hh]
            m_new = jnp.maximum(m_prev, jnp.max(s, axis=-1, keepdims=True))
            alpha = jnp.exp(m_prev - m_new)
            p = jnp.exp(s - m_new).astype(BF16)
            acc_s[hh] = alpha * acc_s[hh] + jnp.dot(p, v_ext, preferred_element_type=F32)
            m_s[hh] = m_new

    def body(j, carry):
        step(j, False)
        return carry

    last = (i * tq) // DA_TK
    lax.fori_loop(0, last, body, 0)
    step(last, True)

    lamv = lamv_ref[...]
    lam = (jnp.exp(jnp.sum(lamv[0:1] * lamv[1:2], axis=-1, keepdims=True))
           - jnp.exp(jnp.sum(lamv[2:3] * lamv[3:4], axis=-1, keepdims=True)) + LAM_INIT)
    for hh in range(DA_HEADS_PER_STEP):
        acc = acc_s[hh]
        o = acc[:, 0:DA_V_DIM] / acc[:, DA_V_DIM:DA_V_DIM + 1]
        a = o[:tq] - lam * o[tq:]
        y = a * lax.rsqrt(jnp.mean(a * a, axis=-1, keepdims=True) + SUBLN_EPS) * g_ref[...]
        o_ref[:, hh * LANES:(hh + 1) * LANES] = (y * (1.0 - LAM_INIT)).astype(BF16)


def _diff_attention(q0, q1, k, v, slopes, lamv, subln_g):
    b, s, _ = q0.shape
    assert DA_TK % DA_TQ == 0 and s % DA_TK == 0
    hw = DA_HEADS_PER_STEP * LANES
    qspec = pl.BlockSpec((None, DA_TQ, hw), lambda bi, hi, qi: (bi, qi, hi))
    kvspec = pl.BlockSpec((None, s, hw), lambda bi, hi, qi: (bi, 0, hi))
    return pl.pallas_call(
        _da_kernel,
        grid=(b, DA_HEADS // DA_HEADS_PER_STEP, s // DA_TQ),
        in_specs=[pl.BlockSpec(memory_space=pltpu.SMEM),
                  pl.BlockSpec((4, DA_QK_DIM), lambda bi, hi, qi: (0, 0)),
                  pl.BlockSpec((1, DA_V_DIM), lambda bi, hi, qi: (0, 0)),
                  qspec, qspec, kvspec, kvspec],
        out_specs=qspec,
        out_shape=jax.ShapeDtypeStruct((b, s, DA_WIDTH), BF16),
        scratch_shapes=[pltpu.VMEM((DA_HEADS_PER_STEP, 2 * DA_TQ, 1), F32),
                        pltpu.VMEM((DA_HEADS_PER_STEP, 2 * DA_TQ, 2 * DA_V_DIM), F32)],
        compiler_params=pltpu.CompilerParams(dimension_semantics=("arbitrary",) * 3, vmem_limit_bytes=VMEM_LIMIT),
        name="diff_attention",
    )(slopes, lamv, subln_g, q0, q1, k, v)


def _mlstm_kernel(tril_ref, cw_ref, cb_ref, ng_ref, q_ref, k_ref, v_ref, og_ref, gate_ref, out_ref,
                  xbuf, cn_s, m_s):
    c = pl.program_id(1)
    L = q_ref.shape[0]

    @pl.when(c == 0)
    def _():
        xbuf[0:SUBLANES, :] = jnp.zeros((SUBLANES, 2 * ML_WIDTH), F32)
        cn_s[...] = jnp.zeros(cn_s.shape, F32)
        m_s[...] = jnp.full(m_s.shape, NEG_INF, F32)

    @pl.when(c > 0)
    def _():
        xbuf[0:SUBLANES, :] = xbuf[L:L + SUBLANES, :]

    xbuf[SUBLANES:SUBLANES + L, 0:ML_WIDTH] = q_ref[...]
    xbuf[SUBLANES:SUBLANES + L, ML_WIDTH:2 * ML_WIDTH] = k_ref[...]
    y = cb_ref[...] + cw_ref[0:1, :] * xbuf[SUBLANES - 3:SUBLANES - 3 + L, :]
    for j in range(1, ML_CONV):
        y = y + cw_ref[j:j + 1, :] * xbuf[SUBLANES - 3 + j:SUBLANES - 3 + j + L, :]
    qk = y * jax.nn.sigmoid(y)

    gates = gate_ref[...]
    i_log = gates[:, 0:LANES]
    f_log = jax.nn.log_sigmoid(gates[:, LANES:2 * LANES])
    b_all = jnp.dot(tril_ref[...], f_log, preferred_element_type=F32,
                    precision=lax.Precision.HIGHEST)
    r_all = i_log - b_all
    r_all_t = r_all.T
    row = lax.broadcasted_iota(jnp.int32, (L, L), 0)
    col = lax.broadcasted_iota(jnp.int32, (L, L), 1)
    tri = col <= row
    one_col = jnp.where(lax.broadcasted_iota(jnp.int32, (L, LANES), 1) == 0, 1.0, 0.0).astype(BF16)

    for h in range(ML_HEADS):
        hs = slice(h * ML_HEAD_DIM, (h + 1) * ML_HEAD_DIM)
        b_col = b_all[:, h:h + 1]
        r_col = r_all[:, h:h + 1]
        r_row = r_all_t[h:h + 1, :]
        m_prev = m_s[h:h + 1, 0:1]
        d = jnp.where(tri, b_col + r_row, NEG_INF)
        inter = b_col + m_prev
        mt = jnp.maximum(inter, jnp.max(d, axis=-1, keepdims=True))
        w_intra = jnp.exp(d - mt)
        w_inter = jnp.exp(inter - mt)
        qb = qk[:, hs].astype(BF16)
        kf = qk[:, ML_WIDTH + h * ML_HEAD_DIM:ML_WIDTH + (h + 1) * ML_HEAD_DIM] * (ML_HEAD_DIM ** -0.5)
        s = _nt_dot(qb, kf.astype(BF16)) * w_intra
        v_ext = jnp.concatenate([v_ref[:, hs], one_col], axis=1)
        cn = cn_s[h]
        num_ext = (w_inter * jnp.dot(qb, cn.astype(BF16), preferred_element_type=F32)
                   + jnp.dot(s.astype(BF16), v_ext, preferred_element_type=F32))
        num = num_ext[:, 0:ML_HEAD_DIM]
        den = num_ext[:, ML_HEAD_DIM:ML_HEAD_DIM + 1]
        hh = num / jnp.maximum(jnp.abs(den), jnp.exp(-mt))
        m_new = mt[L - 1:L, :]
        b_last = b_col[L - 1:L, :]
        ws = jnp.exp(b_last + r_col - m_new)
        decay = jnp.exp(b_last + m_prev - m_new)
        kw_t = (kf * ws).T.astype(BF16)
        cn_s[h] = decay * cn + jnp.dot(kw_t, v_ext, preferred_element_type=F32)
        m_s[h:h + 1, :] = jnp.broadcast_to(m_new, (1, LANES))
        hn = hh * lax.rsqrt(jnp.mean(hh * hh, axis=-1, keepdims=True) + NORM_EPS) * ng_ref[:, hs]
        out_ref[:, hs] = (hn * jax.nn.sigmoid(og_ref[:, hs])).astype(BF16)


def _mlstm(ml_q, ml_k, ml_v, ml_o, gates, conv_w, conv_b, norm_g):
    b, s, _ = ml_q.shape
    L = min(ML_CHUNK, s)
    tril = jnp.tril(jnp.ones((L, L), F32))
    fixed = lambda bi, ci: (0, 0)
    seq = lambda w: pl.BlockSpec((None, L, w), lambda bi, ci: (bi, ci, 0))
    return pl.pallas_call(
        _mlstm_kernel,
        grid=(b, s // L),
        in_specs=[pl.BlockSpec((L, L), fixed),
                  pl.BlockSpec((ML_CONV, 2 * ML_WIDTH), fixed),
                  pl.BlockSpec((1, 2 * ML_WIDTH), fixed),
                  pl.BlockSpec((1, ML_WIDTH), fixed),
                  seq(ML_WIDTH), seq(ML_WIDTH), seq(ML_WIDTH), seq(ML_WIDTH), seq(2 * LANES)],
        out_specs=seq(ML_WIDTH),
        out_shape=jax.ShapeDtypeStruct((b, s, ML_WIDTH), BF16),
        scratch_shapes=[pltpu.VMEM((L + 2 * SUBLANES, 2 * ML_WIDTH), F32),
                        pltpu.VMEM((ML_HEADS, ML_HEAD_DIM, 2 * ML_HEAD_DIM), F32),
                        pltpu.VMEM((SUBLANES, LANES), F32)],
        compiler_params=pltpu.CompilerParams(dimension_semantics=("arbitrary",) * 2, vmem_limit_bytes=VMEM_LIMIT),
        name="mlstm",
    )(tril, conv_w, conv_b, norm_g, ml_q, ml_k, ml_v, ml_o, gates)


def _extract_top(s, n, out_ref, lead):
    rows = s.shape[0]
    iota = lax.broadcasted_iota(jnp.int32, s.shape, 0)
    for k in range(n):
        mx = jnp.max(s, axis=0, keepdims=True)
        out_ref[lead, k:k + 1, :] = mx
        if k + 1 < n:
            first = jnp.min(jnp.where(s == mx, iota, rows), axis=0, keepdims=True)
            s = jnp.where(iota == first, -jnp.inf, s)


def _mid_kernel(a_ref, hm_ref, x_ref, wout_ref, g2_ref, wpq_ref, sk_ref,
                x1_ref, xnt_ref, st_ref, meta_ref, q_s, vals_s, cand_s):
    x1 = (x_ref[...]
          + jnp.dot(a_ref[...], wout_ref[0:DA_WIDTH, :], preferred_element_type=F32)
          + jnp.dot(hm_ref[...], wout_ref[DA_WIDTH:DA_WIDTH + ML_WIDTH, :], preferred_element_type=F32))
    x1_ref[...] = x1
    xn = x1 * lax.rsqrt(jnp.mean(x1 * x1, axis=-1, keepdims=True) + NORM_EPS) * g2_ref[...]
    xnt_ref[...] = xn.T.astype(BF16)
    q = jnp.dot(xn.astype(BF16), wpq_ref[...], preferred_element_type=F32).astype(BF16)
    for hc in range(2 * PEER_HEADS):
        q_s[hc] = q[:, hc * PEER_HALF:(hc + 1) * PEER_HALF]
    tt = x1.shape[0]
    cand_s[N_CAND:N_CAND_PAD, :] = jnp.full((N_CAND_PAD - N_CAND, tt), -jnp.inf, F32)

    def head(h, carry):
        for c in range(2):
            hc = 2 * h + c
            s = _nt_dot(sk_ref[hc], q_s[hc])
            st_ref[hc] = s
            _extract_top(s, N_EXTRACT, vals_s, c)
        off = 0
        for i, cnt in enumerate(CAND_COUNTS):
            cand_s[off:off + cnt, :] = vals_s[0, i:i + 1, :] + vals_s[1, 0:cnt, :]
            off += cnt
        _extract_top(cand_s[...], N_EXTRACT, vals_s, 2)
        top = vals_s[2, 0:PEER_TOPK, :]
        t0 = top[0:1, :]
        lse = t0 + jnp.log(jnp.sum(jnp.exp(top - t0), axis=0, keepdims=True))
        thr = top[PEER_TOPK - 1:PEER_TOPK, :]
        c1 = vals_s[0, 0:1, :]
        tie = jnp.where(vals_s[2, PEER_TOPK:PEER_TOPK + 1, :] == thr, 1.0, 0.0)
        meta_ref[h, 0:1, :] = thr
        meta_ref[h, 1:2, :] = c1
        meta_ref[h, 2:3, :] = lse - c1
        meta_ref[h, 3:4, :] = tie
        meta_ref[h, 4:SUBLANES, :] = jnp.zeros((SUBLANES - 4, tt), F32)
        return carry

    lax.fori_loop(0, PEER_HEADS, head, 0)


def _mid(a, hm, x2, w_out, norm2_g, w_pq, sub_keys):
    t = x2.shape[0]
    tt = min(MID_TOKENS, t)
    row = lambda i: (i, 0)
    fixed = lambda i: (0, 0)
    return pl.pallas_call(
        _mid_kernel,
        grid=(t // tt,),
        in_specs=[pl.BlockSpec((tt, DA_WIDTH), row),
                  pl.BlockSpec((tt, ML_WIDTH), row),
                  pl.BlockSpec((tt, D_MODEL), row),
                  pl.BlockSpec((D_MODEL, D_MODEL), fixed),
                  pl.BlockSpec((1, D_MODEL), fixed),
                  pl.BlockSpec((D_MODEL, 2 * PEER_HEADS * PEER_HALF), fixed),
                  pl.BlockSpec((2 * PEER_HEADS, PEER_NKEYS, PEER_HALF), lambda i: (0, 0, 0))],
        out_specs=[pl.BlockSpec((tt, D_MODEL), row),
                   pl.BlockSpec((D_MODEL, tt), lambda i: (0, i)),
                   pl.BlockSpec((2 * PEER_HEADS, PEER_NKEYS, tt), lambda i: (0, 0, i)),
                   pl.BlockSpec((PEER_HEADS, SUBLANES, tt), lambda i: (0, 0, i))],
        out_shape=[jax.ShapeDtypeStruct((t, D_MODEL), F32),
                   jax.ShapeDtypeStruct((D_MODEL, t), BF16),
                   jax.ShapeDtypeStruct((2 * PEER_HEADS, PEER_NKEYS, t), F32),
                   jax.ShapeDtypeStruct((PEER_HEADS, SUBLANES, t), F32)],
        scratch_shapes=[pltpu.VMEM((2 * PEER_HEADS, tt, PEER_HALF), BF16),
                        pltpu.VMEM((3, 3 * SUBLANES, tt), F32),
                        pltpu.VMEM((N_CAND_PAD, tt), F32)],
        compiler_params=pltpu.CompilerParams(dimension_semantics=("arbitrary",), vmem_limit_bytes=VMEM_LIMIT),
        name="mid",
    )(a, hm, x2, w_out, norm2_g, w_pq, sub_keys)


def _gelu(x):
    return 0.5 * x * (1.0 + lax.erf(x * np.float32(np.sqrt(0.5))))


def _peer_kernel(xnt_ref, u_ref, vt_ref, st_ref, meta_ref, x1_ref, fg_ref, y_ref, e1_s, e2_s, g_s, acc_s):
    e = pl.program_id(1)
    tt = xnt_ref.shape[1]

    @pl.when(e == 0)
    def _():
        for h in range(PEER_HEADS):
            e1_s[h] = jnp.exp(st_ref[2 * h] - meta_ref[h, 1:2, :])
            e2_s[h] = jnp.exp(st_ref[2 * h + 1] - meta_ref[h, 2:3, :])
        acc_s[...] = jnp.zeros(acc_s.shape, F32)

    act_t = jnp.dot(u_ref[...], xnt_ref[...], preferred_element_type=F32)
    for j in range(PEER_ECHUNK // PEER_NKEYS):
        a_idx = e * (PEER_ECHUNK // PEER_NKEYS) + j
        js = slice(j * PEER_NKEYS, (j + 1) * PEER_NKEYS)
        s1_rows = [st_ref[2 * h, pl.ds(a_idx, 1), :] for h in range(PEER_HEADS)]
        e1_rows = [e1_s[h, pl.ds(a_idx, 1), :] for h in range(PEER_HEADS)]
        for lt in range(tt // LANES):
            ls = slice(lt * LANES, (lt + 1) * LANES)
            w = None
            for h in range(PEER_HEADS):
                sel = (s1_rows[h][:, ls] + st_ref[2 * h + 1, :, ls]) >= meta_ref[h, 0:1, ls]
                wh = jnp.where(sel, e2_s[h, :, ls], 0.0) * e1_rows[h][:, ls]
                w = wh if w is None else w + wh
            g_s[js, ls] = (_gelu(act_t[js, ls]) * w).astype(BF16)
    acc_s[...] += jnp.dot(vt_ref[...], g_s[...], preferred_element_type=F32)

    @pl.when(e == pl.num_programs(1) - 1)
    def _():
        x2 = x1_ref[...] + acc_s[...].T
        y_ref[...] = x2 * lax.rsqrt(jnp.mean(x2 * x2, axis=-1, keepdims=True) + NORM_EPS) * fg_ref[...]


def _peer(xnt, u_bf, vt_bf, st, meta, x1, final_g):
    t = x1.shape[0]
    tt = min(PEER_TOKENS, t)
    return pl.pallas_call(
        _peer_kernel,
        grid=(t // tt, PEER_EXPERTS // PEER_ECHUNK),
        in_specs=[pl.BlockSpec((D_MODEL, tt), lambda i, e: (0, i)),
                  pl.BlockSpec((PEER_ECHUNK, D_MODEL), lambda i, e: (e, 0)),
                  pl.BlockSpec((D_MODEL, PEER_ECHUNK), lambda i, e: (0, e)),
                  pl.BlockSpec((2 * PEER_HEADS, PEER_NKEYS, tt), lambda i, e: (0, 0, i)),
                  pl.BlockSpec((PEER_HEADS, SUBLANES, tt), lambda i, e: (0, 0, i)),
                  pl.BlockSpec((tt, D_MODEL), lambda i, e: (i, 0)),
                  pl.BlockSpec((1, D_MODEL), lambda i, e: (0, 0))],
        out_specs=pl.BlockSpec((tt, D_MODEL), lambda i, e: (i, 0)),
        out_shape=jax.ShapeDtypeStruct((t, D_MODEL), F32),
        scratch_shapes=[pltpu.VMEM((PEER_HEADS, PEER_NKEYS, tt), F32),
                        pltpu.VMEM((PEER_HEADS, PEER_NKEYS, tt), F32),
                        pltpu.VMEM((PEER_ECHUNK, tt), BF16),
                        pltpu.VMEM((D_MODEL, tt), F32)],
        compiler_params=pltpu.CompilerParams(dimension_semantics=("arbitrary",) * 2, vmem_limit_bytes=VMEM_LIMIT),
        name="peer_experts",
    )(xnt, u_bf, vt_bf, st, meta, x1, final_g)


def _alibi_slopes(n):
    start = 2.0 ** (-8.0 / n)
    return jnp.asarray(np.array([start ** (h + 1) for h in range(n)], dtype=np.float32))


def kernel(x, norm1_g, w_in, conv_w, conv_b, b_igate, b_fgate, lam_q1, lam_k1, lam_q2, lam_k2, da_subln_g, ml_norm_g, w_out, norm2_g, w_pq, sub_keys, u_emb, v_emb, final_g):
    bsz, seq, d = x.shape
    t = bsz * seq
    x2 = x.reshape(t, d)
    l = 0
    w_main = w_in[l][:, :N_MAIN].astype(BF16)
    gate_pad = jnp.zeros((d, LANES - ML_HEADS), F32)
    w_gate = jnp.concatenate([w_in[l][:, N_MAIN:N_MAIN + ML_HEADS], gate_pad,
                              w_in[l][:, N_MAIN + ML_HEADS:], gate_pad], axis=1).astype(BF16)
    bias_pad = jnp.zeros((LANES - ML_HEADS,), F32)
    b_gate = jnp.concatenate([b_igate[l], bias_pad, b_fgate[l], bias_pad])[None, :]
    lamv = jnp.stack([lam_q1[l], lam_k1[l], lam_q2[l], lam_k2[l]])

    q0, q1, da_k, da_v, ml_q, ml_k, ml_v, ml_o, gates = _inproj(x2, norm1_g[l][None, :], w_main, w_gate, b_gate)
    seq3 = lambda z: z.reshape(bsz, seq, z.shape[-1])
    a = _diff_attention(seq3(q0), seq3(q1), seq3(da_k), seq3(da_v), _alibi_slopes(DA_HEADS), lamv,
                        da_subln_g[l][None, :])
    hm = _mlstm(seq3(ml_q), seq3(ml_k), seq3(ml_v), seq3(ml_o), seq3(gates),
                conv_w[l], conv_b[l][None, :], ml_norm_g[l][None, :])
    x1, xnt, st, meta = _mid(a.reshape(t, DA_WIDTH), hm.reshape(t, ML_WIDTH), x2, w_out[l].astype(BF16),
                             norm2_g[l][None, :], w_pq[l].astype(BF16),
                             sub_keys[l].reshape(2 * PEER_HEADS, PEER_NKEYS, PEER_HALF).astype(BF16))
    y = _peer(xnt, u_emb[l].astype(BF16), v_emb[l].T.astype(BF16), st, meta, x1, final_g[None, :])
    return y.reshape(bsz, seq, d)
```

```python
import functools

import numpy as np
import jax
import jax.numpy as jnp
from jax import lax
from jax.experimental import pallas as pl
from jax.experimental.pallas import tpu as pltpu

F32 = jnp.float32
BF16 = jnp.bfloat16

D_MODEL = 1024
DA_HEADS = 4
DA_QK_DIM = 64
DA_V_DIM = 128
DA_WIDTH = 512
ML_HEADS = 4
ML_HEAD_DIM = 128
ML_WIDTH = 512
ML_CONV = 4
N_MAIN = 7 * 512
PEER_HEADS = 8
PEER_NKEYS = 128
PEER_EXPERTS = PEER_NKEYS * PEER_NKEYS
PEER_HALF = 128
PEER_TOPK = 16
NORM_EPS = 1e-6
SUBLN_EPS = 1e-5
NEG_INF = -1e30
LAM_INIT = 0.2
LANES = 128
SUBLANES = 8

IN_TOKENS = 512
DA_TQ = 256
DA_TK = 512
DA_HEADS_PER_STEP = 2
ML_CHUNK = 256
MID_TOKENS = 256
PEER_TOKENS = 512
PEER_ECHUNK = 1024
PEER_PIECE = 256
PEER_UNIT_TOKENS = 256
CAND_COUNTS = tuple(PEER_TOPK // (i + 1) for i in range(PEER_TOPK))
CAND_OFFSETS = tuple(sum(CAND_COUNTS[:i]) for i in range(PEER_TOPK))
N_CAND = sum(CAND_COUNTS)
N_CAND_PAD = -(-N_CAND // SUBLANES) * SUBLANES
VMEM_LIMIT = 56 * 1024 * 1024


def _nt_dot(a, b):
    return lax.dot_general(a, b, (((1,), (1,)), ((), ())), preferred_element_type=F32)


def _inproj_kernel(x_ref, g_ref, w_ref, wg_ref, bg_ref,
                   q0_ref, q1_ref, dak_ref, dav_ref, mlq_ref, mlk_ref, mlv_ref, mlo_ref, gate_ref):
    x = x_ref[...]
    h = x * lax.rsqrt(jnp.mean(x * x, axis=-1, keepdims=True) + NORM_EPS) * g_ref[...]
    hb = h.astype(BF16)

    def mm(group):
        return jnp.dot(hb, w_ref[:, group * 512:(group + 1) * 512], preferred_element_type=F32)

    q = mm(0) * (DA_QK_DIM ** -0.5)
    lane = lax.broadcasted_iota(jnp.int32, q.shape, 1)
    first = (lane % (2 * DA_QK_DIM)) < DA_QK_DIM
    q0_ref[...] = jnp.where(first, q, 0.0).astype(BF16)
    q1_ref[...] = jnp.where(first, 0.0, q).astype(BF16)
    dak_ref[...] = mm(1).astype(BF16)
    dav_ref[...] = mm(2).astype(BF16)
    mlq_ref[...] = mm(3)
    mlk_ref[...] = mm(4)
    mlv_ref[...] = mm(5).astype(BF16)
    mlo_ref[...] = mm(6)
    gate_ref[...] = jnp.dot(hb, wg_ref[...], preferred_element_type=F32) + bg_ref[...]


def _inproj(x2, norm1_g, w_main, w_gate, b_gate):
    t = x2.shape[0]
    tt = min(IN_TOKENS, t)
    row = lambda i: (i, 0)
    fixed = lambda i: (0, 0)
    wide = lambda dt: jax.ShapeDtypeStruct((t, 512), dt)
    return pl.pallas_call(
        _inproj_kernel,
        grid=(t // tt,),
        in_specs=[pl.BlockSpec((tt, D_MODEL), row),
                  pl.BlockSpec((1, D_MODEL), fixed),
                  pl.BlockSpec((D_MODEL, N_MAIN), fixed),
                  pl.BlockSpec((D_MODEL, 2 * LANES), fixed),
                  pl.BlockSpec((1, 2 * LANES), fixed)],
        out_specs=[pl.BlockSpec((tt, 512), row)] * 8 + [pl.BlockSpec((tt, 2 * LANES), row)],
        out_shape=[wide(BF16), wide(BF16), wide(BF16), wide(BF16), wide(F32), wide(F32), wide(BF16), wide(F32),
                   jax.ShapeDtypeStruct((t, 2 * LANES), F32)],
        compiler_params=pltpu.CompilerParams(dimension_semantics=("arbitrary",), vmem_limit_bytes=VMEM_LIMIT),
        name="inproj",
    )(x2, norm1_g, w_main, w_gate, b_gate)


def _da_kernel(slope_ref, lamv_ref, g_ref, q0_ref, q1_ref, k_ref, v_ref, o_ref, m_s, acc_s):
    hp = pl.program_id(1)
    i = pl.program_id(2)
    tq = q0_ref.shape[0]
    m_s[...] = jnp.full(m_s.shape, NEG_INF, F32)
    acc_s[...] = jnp.zeros(acc_s.shape, F32)
    one_col = jnp.where(lax.broadcasted_iota(jnp.int32, (DA_TK, LANES), 1) == 0, 1.0, 0.0).astype(BF16)

    def step(j, masked):
        kstart = pl.multiple_of(j * DA_TK, DA_TK)
        krel = (kstart - i * tq + lax.broadcasted_iota(jnp.int32, (1, DA_TK), 1)).astype(F32)
        for hh in range(DA_HEADS_PER_STEP):
            hs = slice(hh * LANES, (hh + 1) * LANES)
            q2 = jnp.concatenate([q0_ref[:, hs], q1_ref[:, hs]], axis=0)
            kj = k_ref[pl.ds(kstart, DA_TK), hs]
            v_ext = jnp.concatenate([v_ref[pl.ds(kstart, DA_TK), hs], one_col], axis=1)
            s = _nt_dot(q2, kj) + slope_ref[hp * DA_HEADS_PER_STEP + hh] * krel
            if masked:
                row = lax.broadcasted_iota(jnp.int32, s.shape, 0)
                col = lax.broadcasted_iota(jnp.int32, s.shape, 1)
                rq = jnp.where(row >= tq, row - tq, row)
                s = jnp.where(col + (kstart - i * tq) <= rq, s, NEG_INF)
            m_prev = m_s[hh]
            m_new = jnp.maximum(m_prev, jnp.max(s, axis=-1, keepdims=True))
            alpha = jnp.exp(m_prev - m_new)
            p = jnp.exp(s - m_new).astype(BF16)
            acc_s[hh] = alpha * acc_s[hh] + jnp.dot(p, v_ext, preferred_element_type=F32)
            m_s[hh] = m_new

    def body(j, carry):
        step(j, False)
        return carry

    last = (i * tq) // DA_TK
    lax.fori_loop(0, last, body, 0)
    step(last, True)

    lamv = lamv_ref[...]
    lam = (jnp.exp(jnp.sum(lamv[0:1] * lamv[1:2], axis=-1, keepdims=True))
           - jnp.exp(jnp.sum(lamv[2:3] * lamv[3:4], axis=-1, keepdims=True)) + LAM_INIT)
    for hh in range(DA_HEADS_PER_STEP):
        acc = acc_s[hh]
        o = acc[:, 0:DA_V_DIM] / acc[:, DA_V_DIM:DA_V_DIM + 1]
        a = o[:tq] - lam * o[tq:]
        y = a * lax.rsqrt(jnp.mean(a * a, axis=-1, keepdims=True) + SUBLN_EPS) * g_ref[...]
        o_ref[:, hh * LANES:(hh + 1) * LANES] = (y * (1.0 - LAM_INIT)).astype(BF16)


def _diff_attention(q0, q1, k, v, slopes, lamv, subln_g):
    b, s, _ = q0.shape
    assert DA_TK % DA_TQ == 0 and s % DA_TK == 0
    hw = DA_HEADS_PER_STEP * LANES
    qspec = pl.BlockSpec((None, DA_TQ, hw), lambda bi, hi, qi: (bi, qi, hi))
    kvspec = pl.BlockSpec((None, s, hw), lambda bi, hi, qi: (bi, 0, hi))
    return pl.pallas_call(
        _da_kernel,
        grid=(b, DA_HEADS // DA_HEADS_PER_STEP, s // DA_TQ),
        in_specs=[pl.BlockSpec(memory_space=pltpu.SMEM),
                  pl.BlockSpec((4, DA_QK_DIM), lambda bi, hi, qi: (0, 0)),
                  pl.BlockSpec((1, DA_V_DIM), lambda bi, hi, qi: (0, 0)),
                  qspec, qspec, kvspec, kvspec],
        out_specs=qspec,
        out_shape=jax.ShapeDtypeStruct((b, s, DA_WIDTH), BF16),
        scratch_shapes=[pltpu.VMEM((DA_HEADS_PER_STEP, 2 * DA_TQ, 1), F32),
                        pltpu.VMEM((DA_HEADS_PER_STEP, 2 * DA_TQ, 2 * DA_V_DIM), F32)],
        compiler_params=pltpu.CompilerParams(dimension_semantics=("arbitrary",) * 3, vmem_limit_bytes=VMEM_LIMIT),
        name="diff_attention",
    )(slopes, lamv, subln_g, q0, q1, k, v)


def _mlstm_kernel(tril_ref, cw_ref, cb_ref, ng_ref, q_ref, k_ref, v_ref, og_ref, gate_ref, out_ref,
                  xbuf, cn_s, m_s):
    c = pl.program_id(1)
    L = q_ref.shape[0]

    @pl.when(c == 0)
    def _():
        xbuf[0:SUBLANES, :] = jnp.zeros((SUBLANES, 2 * ML_WIDTH), F32)
        cn_s[...] = jnp.zeros(cn_s.shape, F32)
        m_s[...] = jnp.full(m_s.shape, NEG_INF, F32)

    @pl.when(c > 0)
    def _():
        xbuf[0:SUBLANES, :] = xbuf[L:L + SUBLANES, :]

    xbuf[SUBLANES:SUBLANES + L, 0:ML_WIDTH] = q_ref[...]
    xbuf[SUBLANES:SUBLANES + L, ML_WIDTH:2 * ML_WIDTH] = k_ref[...]
    y = cb_ref[...] + cw_ref[0:1, :] * xbuf[SUBLANES - 3:SUBLANES - 3 + L, :]
    for j in range(1, ML_CONV):
        y = y + cw_ref[j:j + 1, :] * xbuf[SUBLANES - 3 + j:SUBLANES - 3 + j + L, :]
    qk = y * jax.nn.sigmoid(y)

    gates = gate_ref[...]
    i_log = gates[:, 0:LANES]
    f_log = jax.nn.log_sigmoid(gates[:, LANES:2 * LANES])
    b_all = jnp.dot(tril_ref[...], f_log, preferred_element_type=F32,
                    precision=lax.Precision.HIGHEST)
    r_all = i_log - b_all
    r_all_t = r_all.T
    row = lax.broadcasted_iota(jnp.int32, (L, L), 0)
    col = lax.broadcasted_iota(jnp.int32, (L, L), 1)
    tri = col <= row
    one_col = jnp.where(lax.broadcasted_iota(jnp.int32, (L, LANES), 1) == 0, 1.0, 0.0).astype(BF16)

    for h in range(ML_HEADS):
        hs = slice(h * ML_HEAD_DIM, (h + 1) * ML_HEAD_DIM)
        b_col = b_all[:, h:h + 1]
        r_col = r_all[:, h:h + 1]
        r_row = r_all_t[h:h + 1, :]
        m_prev = m_s[h:h + 1, 0:1]
        d = jnp.where(tri, b_col + r_row, NEG_INF)
        inter = b_col + m_prev
        mt = jnp.maximum(inter, jnp.max(d, axis=-1, keepdims=True))
        w_intra = jnp.exp(d - mt)
        w_inter = jnp.exp(inter - mt)
        qb = qk[:, hs].astype(BF16)
        kf = qk[:, ML_WIDTH + h * ML_HEAD_DIM:ML_WIDTH + (h + 1) * ML_HEAD_DIM] * (ML_HEAD_DIM ** -0.5)
        s = _nt_dot(qb, kf.astype(BF16)) * w_intra
        v_ext = jnp.concatenate([v_ref[:, hs], one_col], axis=1)
        cn = cn_s[h]
        num_ext = (w_inter * jnp.dot(qb, cn.astype(BF16), preferred_element_type=F32)
                   + jnp.dot(s.astype(BF16), v_ext, preferred_element_type=F32))
        num = num_ext[:, 0:ML_HEAD_DIM]
        den = num_ext[:, ML_HEAD_DIM:ML_HEAD_DIM + 1]
        hh = num / jnp.maximum(jnp.abs(den), jnp.exp(-mt))
        m_new = mt[L - 1:L, :]
        b_last = b_col[L - 1:L, :]
        ws = jnp.exp(b_last + r_col - m_new)
        decay = jnp.exp(b_last + m_prev - m_new)
        kw_t = (kf * ws).T.astype(BF16)
        cn_s[h] = decay * cn + jnp.dot(kw_t, v_ext, preferred_element_type=F32)
        m_s[h:h + 1, :] = jnp.broadcast_to(m_new, (1, LANES))
        hn = hh * lax.rsqrt(jnp.mean(hh * hh, axis=-1, keepdims=True) + NORM_EPS) * ng_ref[:, hs]
        out_ref[:, hs] = (hn * jax.nn.sigmoid(og_ref[:, hs])).astype(BF16)


def _mlstm(ml_q, ml_k, ml_v, ml_o, gates, conv_w, conv_b, norm_g):
    b, s, _ = ml_q.shape
    L = min(ML_CHUNK, s)
    tril = jnp.tril(jnp.ones((L, L), F32))
    fixed = lambda bi, ci: (0, 0)
    seq = lambda w: pl.BlockSpec((None, L, w), lambda bi, ci: (bi, ci, 0))
    return pl.pallas_call(
        _mlstm_kernel,
        grid=(b, s // L),
        in_specs=[pl.BlockSpec((L, L), fixed),
                  pl.BlockSpec((ML_CONV, 2 * ML_WIDTH), fixed),
                  pl.BlockSpec((1, 2 * ML_WIDTH), fixed),
                  pl.BlockSpec((1, ML_WIDTH), fixed),
                  seq(ML_WIDTH), seq(ML_WIDTH), seq(ML_WIDTH), seq(ML_WIDTH), seq(2 * LANES)],
        out_specs=seq(ML_WIDTH),
        out_shape=jax.ShapeDtypeStruct((b, s, ML_WIDTH), BF16),
        scratch_shapes=[pltpu.VMEM((L + 2 * SUBLANES, 2 * ML_WIDTH), F32),
                        pltpu.VMEM((ML_HEADS, ML_HEAD_DIM, 2 * ML_HEAD_DIM), F32),
                        pltpu.VMEM((SUBLANES, LANES), F32)],
        compiler_params=pltpu.CompilerParams(dimension_semantics=("arbitrary",) * 2, vmem_limit_bytes=VMEM_LIMIT),
        name="mlstm",
    )(tril, conv_w, conv_b, norm_g, ml_q, ml_k, ml_v, ml_o, gates)


def _extract_top_exact(s, n, out_ref, lead):
    rows = s.shape[0]
    iota = lax.broadcasted_iota(jnp.int32, s.shape, 0)
    for k in range(n):
        mx = jnp.max(s, axis=0, keepdims=True)
        out_ref[lead, k:k + 1, :] = mx
        if k + 1 < n:
            first = jnp.min(jnp.where(s == mx, iota, rows), axis=0, keepdims=True)
            s = jnp.where(iota == first, -jnp.inf, s)


def _extract_top(s, n, out_ref, lead, n_pad=0):
    s_in = s
    for k in range(n):
        mx = jnp.max(s, axis=0, keepdims=True)
        out_ref[lead, k:k + 1, :] = mx
        s = jnp.where(s == mx, -jnp.inf, s)
    removed = jnp.sum(jnp.where(s == -jnp.inf, 1.0, 0.0), axis=0, keepdims=True)
    duplicates = jnp.max(jnp.abs(removed - float(n + n_pad))) > 0.0

    @pl.when(duplicates)
    def _():
        _extract_top_exact(s_in, n, out_ref, lead)


def _mid_kernel(a_ref, hm_ref, x_ref, wout_ref, g2_ref, wpq_ref, sk_ref,
                x1_ref, xnt_ref, need_ref, e1_ref, s2_ref, e2_ref, q_s, vals_s, cand_s):
    x1 = (x_ref[...]
          + jnp.dot(a_ref[...], wout_ref[0:DA_WIDTH, :], preferred_element_type=F32)
          + jnp.dot(hm_ref[...], wout_ref[DA_WIDTH:DA_WIDTH + ML_WIDTH, :], preferred_element_type=F32))
    x1_ref[...] = x1
    xn = x1 * lax.rsqrt(jnp.mean(x1 * x1, axis=-1, keepdims=True) + NORM_EPS) * g2_ref[...]
    xnt_ref[...] = xn.T.astype(BF16)
    q = jnp.dot(xn.astype(BF16), wpq_ref[...], preferred_element_type=F32).astype(BF16)
    for hc in range(2 * PEER_HEADS):
        q_s[hc] = q[:, hc * PEER_HALF:(hc + 1) * PEER_HALF]
    tt = x1.shape[0]
    cand_s[N_CAND:N_CAND_PAD, :] = jnp.full((N_CAND_PAD - N_CAND, tt), -jnp.inf, F32)

    def head(h, carry):
        s1 = _nt_dot(sk_ref[2 * h], q_s[2 * h])
        s2 = _nt_dot(sk_ref[2 * h + 1], q_s[2 * h + 1])
        _extract_top(s1, PEER_TOPK, vals_s, 0)
        _extract_top(s2, PEER_TOPK, vals_s, 1)
        for i, cnt in enumerate(CAND_COUNTS):
            off = CAND_OFFSETS[i]
            cand_s[off:off + cnt, :] = vals_s[0, i:i + 1, :] + vals_s[1, 0:cnt, :]
        _extract_top(cand_s[...], PEER_TOPK, vals_s, 2, n_pad=N_CAND_PAD - N_CAND)
        top = vals_s[2, 0:PEER_TOPK, :]
        t0 = top[0:1, :]
        lse = t0 + jnp.log(jnp.sum(jnp.exp(top - t0), axis=0, keepdims=True))
        thr = top[PEER_TOPK - 1:PEER_TOPK, :]
        need = jnp.full(s1.shape, jnp.inf, F32)
        for i in reversed(range(PEER_TOPK)):
            cnt, off = CAND_COUNTS[i], CAND_OFFSETS[i]
            cut = jnp.min(jnp.where(cand_s[off:off + cnt, :] >= thr, vals_s[1, 0:cnt, :], jnp.inf),
                          axis=0, keepdims=True)
            need = jnp.where(s1 >= vals_s[0, i:i + 1, :], cut, need)
        c1 = vals_s[0, 0:1, :]
        need_ref[h] = need
        e1_ref[h] = jnp.exp(s1 - c1)
        s2_ref[h] = s2
        e2_ref[h] = 0.5 * jnp.exp(s2 - (lse - c1))
        return carry

    lax.fori_loop(0, PEER_HEADS, head, 0)


def _mid(a, hm, x2, w_out, norm2_g, w_pq, sub_keys):
    t = x2.shape[0]
    tt = min(MID_TOKENS, t)
    row = lambda i: (i, 0)
    fixed = lambda i: (0, 0)
    return pl.pallas_call(
        _mid_kernel,
        grid=(t // tt,),
        in_specs=[pl.BlockSpec((tt, DA_WIDTH), row),
                  pl.BlockSpec((tt, ML_WIDTH), row),
                  pl.BlockSpec((tt, D_MODEL), row),
                  pl.BlockSpec((D_MODEL, D_MODEL), fixed),
                  pl.BlockSpec((1, D_MODEL), fixed),
                  pl.BlockSpec((D_MODEL, 2 * PEER_HEADS * PEER_HALF), fixed),
                  pl.BlockSpec((2 * PEER_HEADS, PEER_NKEYS, PEER_HALF), lambda i: (0, 0, 0))],
        out_specs=[pl.BlockSpec((tt, D_MODEL), row),
                   pl.BlockSpec((D_MODEL, tt), lambda i: (0, i))]
                  + [pl.BlockSpec((PEER_HEADS, PEER_NKEYS, tt), lambda i: (0, 0, i))] * 4,
        out_shape=[jax.ShapeDtypeStruct((t, D_MODEL), F32),
                   jax.ShapeDtypeStruct((D_MODEL, t), BF16)]
                  + [jax.ShapeDtypeStruct((PEER_HEADS, PEER_NKEYS, t), F32)] * 4,
        scratch_shapes=[pltpu.VMEM((2 * PEER_HEADS, tt, PEER_HALF), BF16),
                        pltpu.VMEM((3, 2 * SUBLANES, tt), F32),
                        pltpu.VMEM((N_CAND_PAD, tt), F32)],
        compiler_params=pltpu.CompilerParams(dimension_semantics=("arbitrary",), vmem_limit_bytes=VMEM_LIMIT),
        name="mid",
    )(a, hm, x2, w_out, norm2_g, w_pq, sub_keys)


def _gate_unit(key0, act_ref, need_ref, e1_ref, s2_ref, e2_ref, g_ref, row0, th):
    for q in range(PEER_PIECE // PEER_NKEYS):
        need_rows = [need_ref[h, pl.ds(key0 + q, 1), :] for h in range(PEER_HEADS)]
        e1_rows = [e1_ref[h, pl.ds(key0 + q, 1), :] for h in range(PEER_HEADS)]
        for lt in range(PEER_UNIT_TOKENS // LANES):
            la = slice(lt * LANES, (lt + 1) * LANES)
            ls = slice(th * PEER_UNIT_TOKENS + lt * LANES, th * PEER_UNIT_TOKENS + (lt + 1) * LANES)
            w = None
            for h in range(PEER_HEADS):
                wh = jnp.where(s2_ref[h, :, ls] >= need_rows[h][:, ls], e2_ref[h, :, ls], 0.0) * e1_rows[h][:, ls]
                w = wh if w is None else w + wh
            a = act_ref[q * PEER_NKEYS:(q + 1) * PEER_NKEYS, la]
            g_ref[pl.ds(pl.multiple_of(row0 + q * PEER_NKEYS, PEER_NKEYS), PEER_NKEYS), ls] = (
                a * (1.0 + lax.erf(a * np.float32(np.sqrt(0.5)))) * w).astype(BF16)


def _peer_kernel(xnt_ref, u_ref, vt_ref, need_ref, e1_ref, s2_ref, e2_ref, x1_ref, fg_ref, y_ref,
                 ga_s, gb_s, vkeep_s, acc_s, act_s):
    k = pl.program_id(1)

    @pl.when(k == 0)
    def _():
        acc_s[...] = jnp.zeros(acc_s.shape, F32)
        gb_s[...] = jnp.zeros(gb_s.shape, BF16)
        vkeep_s[...] = jnp.zeros(vkeep_s.shape, BF16)

    per_chunk = PEER_ECHUNK // PEER_PIECE
    n_piece = 2 * per_chunk
    keys_per_piece = PEER_PIECE // PEER_NKEYS

    def act_unit(p, th):
        off = pl.multiple_of(jnp.minimum(p, n_piece - 1) * PEER_PIECE, PEER_PIECE)
        act_s[th] = jnp.dot(u_ref[pl.ds(off, PEER_PIECE), :],
                            xnt_ref[:, th * PEER_UNIT_TOKENS:(th + 1) * PEER_UNIT_TOKENS],
                            preferred_element_type=F32)

    act_unit(0, 0)
    for c in range(2):
        g_prev, g_cur = (gb_s, ga_s) if c == 0 else (ga_s, gb_s)

        def piece(r, carry, c=c, g_prev=g_prev, g_cur=g_cur):
            p = c * per_chunk + r
            rows = pl.ds(pl.multiple_of(r * PEER_PIECE, PEER_PIECE), PEER_PIECE)
            v_rows = vkeep_s[rows, :] if c == 0 else vt_ref[rows, 0:PEER_ECHUNK]
            for th in range(2):
                ts = slice(th * PEER_UNIT_TOKENS, (th + 1) * PEER_UNIT_TOKENS)
                act_unit(p + th, 1 - th)
                acc_s[rows, ts] += jnp.dot(v_rows, g_prev[:, ts], preferred_element_type=F32)
                _gate_unit(p * keys_per_piece, act_s.at[th], need_ref, e1_ref, s2_ref, e2_ref, g_cur,
                           r * PEER_PIECE, th)
            return carry

        lax.fori_loop(0, per_chunk, piece, 0)
    vkeep_s[...] = vt_ref[:, PEER_ECHUNK:2 * PEER_ECHUNK]

    @pl.when(k == pl.num_programs(1) - 1)
    def _():
        acc = acc_s[...] + jnp.dot(vkeep_s[...], gb_s[...], preferred_element_type=F32)
        x2 = x1_ref[...] + acc.T
        y_ref[...] = x2 * lax.rsqrt(jnp.mean(x2 * x2, axis=-1, keepdims=True) + NORM_EPS) * fg_ref[...]


def _peer(xnt, u_bf, vt_bf, need, e1, s2, e2, x1, final_g):
    t = x1.shape[0]
    tt = min(PEER_TOKENS, t)
    assert tt == 2 * PEER_UNIT_TOKENS
    step = 2 * PEER_ECHUNK
    step_keys = step // PEER_NKEYS
    key_rows = pl.BlockSpec((PEER_HEADS, step_keys, tt), lambda i, e: (0, e, i))
    all_keys = pl.BlockSpec((PEER_HEADS, PEER_NKEYS, tt), lambda i, e: (0, 0, i))
    return pl.pallas_call(
        _peer_kernel,
        grid=(t // tt, PEER_EXPERTS // step),
        in_specs=[pl.BlockSpec((D_MODEL, tt), lambda i, e: (0, i)),
                  pl.BlockSpec((step, D_MODEL), lambda i, e: (e, 0)),
                  pl.BlockSpec((D_MODEL, step), lambda i, e: (0, e)),
                  key_rows, key_rows, all_keys, all_keys,
                  pl.BlockSpec((tt, D_MODEL), lambda i, e: (i, 0)),
                  pl.BlockSpec((1, D_MODEL), lambda i, e: (0, 0))],
        out_specs=pl.BlockSpec((tt, D_MODEL), lambda i, e: (i, 0)),
        out_shape=jax.ShapeDtypeStruct((t, D_MODEL), F32),
        scratch_shapes=[pltpu.VMEM((PEER_ECHUNK, tt), BF16),
                        pltpu.VMEM((PEER_ECHUNK, tt), BF16),
                        pltpu.VMEM((D_MODEL, PEER_ECHUNK), BF16),
                        pltpu.VMEM((D_MODEL, tt), F32),
                        pltpu.VMEM((2, PEER_PIECE, PEER_UNIT_TOKENS), F32)],
        compiler_params=pltpu.CompilerParams(dimension_semantics=("arbitrary",) * 2, vmem_limit_bytes=VMEM_LIMIT),
        name="peer_experts",
    )(xnt, u_bf, vt_bf, need, e1, s2, e2, x1, final_g)


def _alibi_slopes(n):
    start = 2.0 ** (-8.0 / n)
    return jnp.asarray(np.array([start ** (h + 1) for h in range(n)], dtype=np.float32))


def kernel(x, norm1_g, w_in, conv_w, conv_b, b_igate, b_fgate, lam_q1, lam_k1, lam_q2, lam_k2, da_subln_g, ml_norm_g, w_out, norm2_g, w_pq, sub_keys, u_emb, v_emb, final_g):
    bsz, seq, d = x.shape
    t = bsz * seq
    x2 = x.reshape(t, d)
    l = 0
    w_main = w_in[l][:, :N_MAIN].astype(BF16)
    gate_pad = jnp.zeros((d, LANES - ML_HEADS), F32)
    w_gate = jnp.concatenate([w_in[l][:, N_MAIN:N_MAIN + ML_HEADS], gate_pad,
                              w_in[l][:, N_MAIN + ML_HEADS:], gate_pad], axis=1).astype(BF16)
    bias_pad = jnp.zeros((LANES - ML_HEADS,), F32)
    b_gate = jnp.concatenate([b_igate[l], bias_pad, b_fgate[l], bias_pad])[None, :]
    lamv = jnp.stack([lam_q1[l], lam_k1[l], lam_q2[l], lam_k2[l]])

    q0, q1, da_k, da_v, ml_q, ml_k, ml_v, ml_o, gates = _inproj(x2, norm1_g[l][None, :], w_main, w_gate, b_gate)
    seq3 = lambda z: z.reshape(bsz, seq, z.shape[-1])
    a = _diff_attention(seq3(q0), seq3(q1), seq3(da_k), seq3(da_v), _alibi_slopes(DA_HEADS), lamv,
                        da_subln_g[l][None, :])
    hm = _mlstm(seq3(ml_q), seq3(ml_k), seq3(ml_v), seq3(ml_o), seq3(gates),
                conv_w[l], conv_b[l][None, :], ml_norm_g[l][None, :])
    x1, xnt, need, e1, s2, e2 = _mid(a.reshape(t, DA_WIDTH), hm.reshape(t, ML_WIDTH), x2, w_out[l].astype(BF16),
                                     norm2_g[l][None, :], w_pq[l].astype(BF16),
                                     sub_keys[l].reshape(2 * PEER_HEADS, PEER_NKEYS, PEER_HALF).astype(BF16))
    y = _peer(xnt, u_emb[l].astype(BF16), v_emb[l].T.astype(BF16), need, e1, s2, e2, x1, final_g[None, :])
    return y.reshape(bsz, seq, d)
```

```python
import functools

import numpy as np
import jax
import jax.numpy as jnp
from jax import lax
from jax.experimental import pallas as pl
from jax.experimental.pallas import tpu as pltpu

F32 = jnp.float32
BF16 = jnp.bfloat16

D_MODEL = 1024
DA_HEADS = 4
DA_QK_DIM = 64
DA_V_DIM = 128
DA_WIDTH = 512
ML_HEADS = 4
ML_HEAD_DIM = 128
ML_WIDTH = 512
ML_CONV = 4
N_MAIN = 7 * 512
PEER_HEADS = 8
PEER_NKEYS = 128
PEER_EXPERTS = PEER_NKEYS * PEER_NKEYS
PEER_HALF = 128
PEER_TOPK = 16
NORM_EPS = 1e-6
SUBLN_EPS = 1e-5
NEG_INF = -1e30
LAM_INIT = 0.2
LANES = 128
SUBLANES = 8

IN_TOKENS = 512
DA_TQ = 256
DA_TK = 512
DA_HEADS_PER_STEP = 2
ML_CHUNK = 256
MID_TOKENS = 256
PEER_TOKENS = 512
PEER_ECHUNK = 1024
PEER_PIECE = 256
PEER_UNIT_TOKENS = 256
CAND_COUNTS = tuple(PEER_TOPK // (i + 1) for i in range(PEER_TOPK))
CAND_OFFSETS = tuple(sum(CAND_COUNTS[:i]) for i in range(PEER_TOPK))
N_CAND = sum(CAND_COUNTS)
N_CAND_PAD = -(-N_CAND // SUBLANES) * SUBLANES
VMEM_LIMIT = 56 * 1024 * 1024


def _nt_dot(a, b):
    return lax.dot_general(a, b, (((1,), (1,)), ((), ())), preferred_element_type=F32)


def _inproj_kernel(x_ref, g_ref, w_ref, wg_ref, bg_ref,
                   q0_ref, q1_ref, dak_ref, dav_ref, mlq_ref, mlk_ref, mlv_ref, mlo_ref, gate_ref):
    x = x_ref[...]
    h = x * lax.rsqrt(jnp.mean(x * x, axis=-1, keepdims=True) + NORM_EPS) * g_ref[...]
    hb = h.astype(BF16)

    def mm(group):
        return jnp.dot(hb, w_ref[:, group * 512:(group + 1) * 512], preferred_element_type=F32)

    q = mm(0) * (DA_QK_DIM ** -0.5)
    lane = lax.broadcasted_iota(jnp.int32, q.shape, 1)
    first = (lane % (2 * DA_QK_DIM)) < DA_QK_DIM
    q0_ref[...] = jnp.where(first, q, 0.0).astype(BF16)
    q1_ref[...] = jnp.where(first, 0.0, q).astype(BF16)
    dak_ref[...] = mm(1).astype(BF16)
    dav_ref[...] = mm(2).astype(BF16)
    mlq_ref[...] = mm(3)
    mlk_ref[...] = mm(4)
    mlv_ref[...] = mm(5).astype(BF16)
    mlo_ref[...] = mm(6)
    gate_ref[...] = jnp.dot(hb, wg_ref[...], preferred_element_type=F32) + bg_ref[...]


def _inproj(x2, norm1_g, w_main, w_gate, b_gate):
    t = x2.shape[0]
    tt = min(IN_TOKENS, t)
    row = lambda i: (i, 0)
    fixed = lambda i: (0, 0)
    wide = lambda dt: jax.ShapeDtypeStruct((t, 512), dt)
    return pl.pallas_call(
        _inproj_kernel,
        grid=(t // tt,),
        in_specs=[pl.BlockSpec((tt, D_MODEL), row),
                  pl.BlockSpec((1, D_MODEL), fixed),
                  pl.BlockSpec((D_MODEL, N_MAIN), fixed),
                  pl.BlockSpec((D_MODEL, 2 * LANES), fixed),
                  pl.BlockSpec((1, 2 * LANES), fixed)],
        out_specs=[pl.BlockSpec((tt, 512), row)] * 8 + [pl.BlockSpec((tt, 2 * LANES), row)],
        out_shape=[wide(BF16), wide(BF16), wide(BF16), wide(BF16), wide(F32), wide(F32), wide(BF16), wide(F32),
                   jax.ShapeDtypeStruct((t, 2 * LANES), F32)],
        compiler_params=pltpu.CompilerParams(dimension_semantics=("arbitrary",), vmem_limit_bytes=VMEM_LIMIT),
        name="inproj",
    )(x2, norm1_g, w_main, w_gate, b_gate)


def _da_kernel(slope_ref, lamv_ref, g_ref, q0_ref, q1_ref, k_ref, v_ref, o_ref, m_s, acc_s, s_s):
    hp = pl.program_id(1)
    i = pl.program_id(2)
    tq = q0_ref.shape[0]
    m_s[...] = jnp.full(m_s.shape, NEG_INF, F32)
    acc_s[...] = jnp.zeros(acc_s.shape, F32)
    one_col = jnp.where(lax.broadcasted_iota(jnp.int32, (DA_TK, LANES), 1) == 0, 1.0, 0.0).astype(BF16)

    def scores(j, slot):
        kstart = pl.multiple_of(j * DA_TK, DA_TK)
        krel = (kstart - i * tq + lax.broadcasted_iota(jnp.int32, (1, DA_TK), 1)).astype(F32)
        for hh in range(DA_HEADS_PER_STEP):
            hs = slice(hh * LANES, (hh + 1) * LANES)
            q2 = jnp.concatenate([q0_ref[:, hs], q1_ref[:, hs]], axis=0)
            s_s[slot, hh] = (_nt_dot(q2, k_ref[pl.ds(kstart, DA_TK), hs])
                             + slope_ref[hp * DA_HEADS_PER_STEP + hh] * krel)

    def softmax_pv(j, slot, masked):
        kstart = pl.multiple_of(j * DA_TK, DA_TK)
        for hh in range(DA_HEADS_PER_STEP):
            hs = slice(hh * LANES, (hh + 1) * LANES)
            v_ext = jnp.concatenate([v_ref[pl.ds(kstart, DA_TK), hs], one_col], axis=1)
            s = s_s[slot, hh]
            if masked:
                row = lax.broadcasted_iota(jnp.int32, s.shape, 0)
                col = lax.broadcasted_iota(jnp.int32, s.shape, 1)
                rq = jnp.where(row >= tq, row - tq, row)
                s = jnp.where(col + (kstart - i * tq) <= rq, s, NEG_INF)
            m_prev = m_s[hh]
            m_new = jnp.maximum(m_prev, jnp.max(s, axis=-1, keepdims=True))
            alpha = jnp.exp(m_prev - m_new)
            p = jnp.exp(s - m_new).astype(BF16)
            acc_s[hh] = alpha * acc_s[hh] + jnp.dot(p, v_ext, preferred_element_type=F32)
            m_s[hh] = m_new

    last = (i * tq) // DA_TK

    def block(j, slot):
        @pl.when(j < last)
        def _():
            scores(j + 1, 1 - slot)
            softmax_pv(j, slot, False)

        @pl.when(j == last)
        def _():
            softmax_pv(j, slot, True)

    def pair(jj, carry):
        block(2 * jj, 0)
        block(2 * jj + 1, 1)
        return carry

    scores(0, 0)
    lax.fori_loop(0, last // 2 + 1, pair, 0)

    lamv = lamv_ref[...]
    lam = (jnp.exp(jnp.sum(lamv[0:1] * lamv[1:2], axis=-1, keepdims=True))
           - jnp.exp(jnp.sum(lamv[2:3] * lamv[3:4], axis=-1, keepdims=True)) + LAM_INIT)
    for hh in range(DA_HEADS_PER_STEP):
        acc = acc_s[hh]
        o = acc[:, 0:DA_V_DIM] / acc[:, DA_V_DIM:DA_V_DIM + 1]
        a = o[:tq] - lam * o[tq:]
        y = a * lax.rsqrt(jnp.mean(a * a, axis=-1, keepdims=True) + SUBLN_EPS) * g_ref[...]
        o_ref[:, hh * LANES:(hh + 1) * LANES] = (y * (1.0 - LAM_INIT)).astype(BF16)


def _diff_attention(q0, q1, k, v, slopes, lamv, subln_g):
    b, s, _ = q0.shape
    assert DA_TK % DA_TQ == 0 and s % DA_TK == 0
    hw = DA_HEADS_PER_STEP * LANES
    qspec = pl.BlockSpec((None, DA_TQ, hw), lambda bi, hi, qi: (bi, qi, hi))
    kvspec = pl.BlockSpec((None, s, hw), lambda bi, hi, qi: (bi, 0, hi))
    return pl.pallas_call(
        _da_kernel,
        grid=(b, DA_HEADS // DA_HEADS_PER_STEP, s // DA_TQ),
        in_specs=[pl.BlockSpec(memory_space=pltpu.SMEM),
                  pl.BlockSpec((4, DA_QK_DIM), lambda bi, hi, qi: (0, 0)),
                  pl.BlockSpec((1, DA_V_DIM), lambda bi, hi, qi: (0, 0)),
                  qspec, qspec, kvspec, kvspec],
        out_specs=qspec,
        out_shape=jax.ShapeDtypeStruct((b, s, DA_WIDTH), BF16),
        scratch_shapes=[pltpu.VMEM((DA_HEADS_PER_STEP, 2 * DA_TQ, 1), F32),
                        pltpu.VMEM((DA_HEADS_PER_STEP, 2 * DA_TQ, 2 * DA_V_DIM), F32),
                        pltpu.VMEM((2, DA_HEADS_PER_STEP, 2 * DA_TQ, DA_TK), F32)],
        compiler_params=pltpu.CompilerParams(dimension_semantics=("arbitrary",) * 3, vmem_limit_bytes=VMEM_LIMIT),
        name="diff_attention",
    )(slopes, lamv, subln_g, q0, q1, k, v)


def _mlstm_kernel(tril_ref, cw_ref, cb_ref, ng_ref, q_ref, k_ref, v_ref, og_ref, gate_ref, out_ref,
                  xbuf, cn_s, m_s):
    c = pl.program_id(1)
    L = q_ref.shape[0]

    @pl.when(c == 0)
    def _():
        xbuf[0:SUBLANES, :] = jnp.zeros((SUBLANES, 2 * ML_WIDTH), F32)
        cn_s[...] = jnp.zeros(cn_s.shape, F32)
        m_s[...] = jnp.full(m_s.shape, NEG_INF, F32)

    @pl.when(c > 0)
    def _():
        xbuf[0:SUBLANES, :] = xbuf[L:L + SUBLANES, :]

    xbuf[SUBLANES:SUBLANES + L, 0:ML_WIDTH] = q_ref[...]
    xbuf[SUBLANES:SUBLANES + L, ML_WIDTH:2 * ML_WIDTH] = k_ref[...]
    y = cb_ref[...] + cw_ref[0:1, :] * xbuf[SUBLANES - 3:SUBLANES - 3 + L, :]
    for j in range(1, ML_CONV):
        y = y + cw_ref[j:j + 1, :] * xbuf[SUBLANES - 3 + j:SUBLANES - 3 + j + L, :]
    qk = y * jax.nn.sigmoid(y)

    gates = gate_ref[...]
    i_log = gates[:, 0:LANES]
    f_log = jax.nn.log_sigmoid(gates[:, LANES:2 * LANES])
    b_all = jnp.dot(tril_ref[...], f_log, preferred_element_type=F32,
                    precision=lax.Precision.HIGHEST)
    r_all = i_log - b_all
    r_all_t = r_all.T
    row = lax.broadcasted_iota(jnp.int32, (L, L), 0)
    col = lax.broadcasted_iota(jnp.int32, (L, L), 1)
    tri = col <= row
    one_col = jnp.where(lax.broadcasted_iota(jnp.int32, (L, LANES), 1) == 0, 1.0, 0.0).astype(BF16)

    for h in range(ML_HEADS):
        hs = slice(h * ML_HEAD_DIM, (h + 1) * ML_HEAD_DIM)
        b_col = b_all[:, h:h + 1]
        r_col = r_all[:, h:h + 1]
        r_row = r_all_t[h:h + 1, :]
        m_prev = m_s[h:h + 1, 0:1]
        d = jnp.where(tri, b_col + r_row, NEG_INF)
        inter = b_col + m_prev
        mt = jnp.maximum(inter, jnp.max(d, axis=-1, keepdims=True))
        w_intra = jnp.exp(d - mt)
        w_inter = jnp.exp(inter - mt)
        qb = qk[:, hs].astype(BF16)
        kf = qk[:, ML_WIDTH + h * ML_HEAD_DIM:ML_WIDTH + (h + 1) * ML_HEAD_DIM] * (ML_HEAD_DIM ** -0.5)
        s = _nt_dot(qb, kf.astype(BF16)) * w_intra
        v_ext = jnp.concatenate([v_ref[:, hs], one_col], axis=1)
        cn = cn_s[h]
        num_ext = (w_inter * jnp.dot(qb, cn.astype(BF16), preferred_element_type=F32)
                   + jnp.dot(s.astype(BF16), v_ext, preferred_element_type=F32))
        num = num_ext[:, 0:ML_HEAD_DIM]
        den = num_ext[:, ML_HEAD_DIM:ML_HEAD_DIM + 1]
        hh = num / jnp.maximum(jnp.abs(den), jnp.exp(-mt))
        m_new = mt[L - 1:L, :]
        b_last = b_col[L - 1:L, :]
        ws = jnp.exp(b_last + r_col - m_new)
        decay = jnp.exp(b_last + m_prev - m_new)
        kw_t = (kf * ws).T.astype(BF16)
        cn_s[h] = decay * cn + jnp.dot(kw_t, v_ext, preferred_element_type=F32)
        m_s[h:h + 1, :] = jnp.broadcast_to(m_new, (1, LANES))
        hn = hh * lax.rsqrt(jnp.mean(hh * hh, axis=-1, keepdims=True) + NORM_EPS) * ng_ref[:, hs]
        out_ref[:, hs] = (hn * jax.nn.sigmoid(og_ref[:, hs])).astype(BF16)


def _mlstm(ml_q, ml_k, ml_v, ml_o, gates, conv_w, conv_b, norm_g):
    b, s, _ = ml_q.shape
    L = min(ML_CHUNK, s)
    tril = jnp.tril(jnp.ones((L, L), F32))
    fixed = lambda bi, ci: (0, 0)
    seq = lambda w: pl.BlockSpec((None, L, w), lambda bi, ci: (bi, ci, 0))
    return pl.pallas_call(
        _mlstm_kernel,
        grid=(b, s // L),
        in_specs=[pl.BlockSpec((L, L), fixed),
                  pl.BlockSpec((ML_CONV, 2 * ML_WIDTH), fixed),
                  pl.BlockSpec((1, 2 * ML_WIDTH), fixed),
                  pl.BlockSpec((1, ML_WIDTH), fixed),
                  seq(ML_WIDTH), seq(ML_WIDTH), seq(ML_WIDTH), seq(ML_WIDTH), seq(2 * LANES)],
        out_specs=seq(ML_WIDTH),
        out_shape=jax.ShapeDtypeStruct((b, s, ML_WIDTH), BF16),
        scratch_shapes=[pltpu.VMEM((L + 2 * SUBLANES, 2 * ML_WIDTH), F32),
                        pltpu.VMEM((ML_HEADS, ML_HEAD_DIM, 2 * ML_HEAD_DIM), F32),
                        pltpu.VMEM((SUBLANES, LANES), F32)],
        compiler_params=pltpu.CompilerParams(dimension_semantics=("arbitrary",) * 2, vmem_limit_bytes=VMEM_LIMIT),
        name="mlstm",
    )(tril, conv_w, conv_b, norm_g, ml_q, ml_k, ml_v, ml_o, gates)


def _extract_top_exact(s, n, out_ref, lead):
    rows = s.shape[0]
    iota = lax.broadcasted_iota(jnp.int32, s.shape, 0)
    for k in range(n):
        mx = jnp.max(s, axis=0, keepdims=True)
        out_ref[lead, k:k + 1, :] = mx
        if k + 1 < n:
            first = jnp.min(jnp.where(s == mx, iota, rows), axis=0, keepdims=True)
            s = jnp.where(iota == first, -jnp.inf, s)


def _extract_top_fast(s, n, out_ref, lead, n_pad=0):
    for k in range(n):
        mx = jnp.max(s, axis=0, keepdims=True)
        out_ref[lead, k:k + 1, :] = mx
        s = jnp.where(s == mx, -jnp.inf, s)
    removed = jnp.sum(jnp.where(s == -jnp.inf, 1.0, 0.0), axis=0, keepdims=True)
    return removed - float(n + n_pad)


def _mid_kernel(a_ref, hm_ref, x_ref, wout_ref, g2_ref, wpq_ref, sk_ref,
                x1_ref, xnt_ref, need_ref, e1_ref, s2_ref, e2_ref, q_s, vals_s, cand_s):
    x1 = (x_ref[...]
          + jnp.dot(a_ref[...], wout_ref[0:DA_WIDTH, :], preferred_element_type=F32)
          + jnp.dot(hm_ref[...], wout_ref[DA_WIDTH:DA_WIDTH + ML_WIDTH, :], preferred_element_type=F32))
    x1_ref[...] = x1
    xn = x1 * lax.rsqrt(jnp.mean(x1 * x1, axis=-1, keepdims=True) + NORM_EPS) * g2_ref[...]
    xnt_ref[...] = xn.T.astype(BF16)
    q = jnp.dot(xn.astype(BF16), wpq_ref[...], preferred_element_type=F32).astype(BF16)
    for hc in range(2 * PEER_HEADS):
        q_s[hc] = q[:, hc * PEER_HALF:(hc + 1) * PEER_HALF]
    tt = x1.shape[0]
    cand_s[N_CAND:N_CAND_PAD, :] = jnp.full((N_CAND_PAD - N_CAND, tt), -jnp.inf, F32)

    def head(h, excess, exact):
        s1 = _nt_dot(sk_ref[2 * h], q_s[2 * h])
        s2 = _nt_dot(sk_ref[2 * h + 1], q_s[2 * h + 1])

        def extract(s, lead, n_pad=0):
            if exact:
                _extract_top_exact(s, PEER_TOPK, vals_s, lead)
                return excess
            return jnp.maximum(excess, _extract_top_fast(s, PEER_TOPK, vals_s, lead, n_pad))

        excess = extract(s1, 0)
        excess = extract(s2, 1)
        for i, cnt in enumerate(CAND_COUNTS):
            off = CAND_OFFSETS[i]
            cand_s[off:off + cnt, :] = vals_s[0, i:i + 1, :] + vals_s[1, 0:cnt, :]
        excess = extract(cand_s[...], 2, N_CAND_PAD - N_CAND)
        top = vals_s[2, 0:PEER_TOPK, :]
        t0 = top[0:1, :]
        lse = t0 + jnp.log(jnp.sum(jnp.exp(top - t0), axis=0, keepdims=True))
        thr = top[PEER_TOPK - 1:PEER_TOPK, :]
        need = jnp.full(s1.shape, jnp.inf, F32)
        for i in reversed(range(PEER_TOPK)):
            cnt, off = CAND_COUNTS[i], CAND_OFFSETS[i]
            cut = jnp.min(jnp.where(cand_s[off:off + cnt, :] >= thr, vals_s[1, 0:cnt, :], jnp.inf),
                          axis=0, keepdims=True)
            need = jnp.where(s1 >= vals_s[0, i:i + 1, :], cut, need)
        c1 = vals_s[0, 0:1, :]
        need_ref[h] = need
        e1_ref[h] = jnp.exp(s1 - c1)
        s2_ref[h] = s2
        e2_ref[h] = 0.5 * jnp.exp(s2 - (lse - c1))
        return excess

    excess = lax.fori_loop(0, PEER_HEADS, functools.partial(head, exact=False), jnp.zeros((1, tt), F32))

    @pl.when(jnp.max(excess) > 0.0)
    def _():
        lax.fori_loop(0, PEER_HEADS, functools.partial(head, exact=True), jnp.zeros((1, tt), F32))


def _mid(a, hm, x2, w_out, norm2_g, w_pq, sub_keys):
    t = x2.shape[0]
    tt = min(MID_TOKENS, t)
    row = lambda i: (i, 0)
    fixed = lambda i: (0, 0)
    return pl.pallas_call(
        _mid_kernel,
        grid=(t // tt,),
        in_specs=[pl.BlockSpec((tt, DA_WIDTH), row),
                  pl.BlockSpec((tt, ML_WIDTH), row),
                  pl.BlockSpec((tt, D_MODEL), row),
                  pl.BlockSpec((D_MODEL, D_MODEL), fixed),
                  pl.BlockSpec((1, D_MODEL), fixed),
                  pl.BlockSpec((D_MODEL, 2 * PEER_HEADS * PEER_HALF), fixed),
                  pl.BlockSpec((2 * PEER_HEADS, PEER_NKEYS, PEER_HALF), lambda i: (0, 0, 0))],
        out_specs=[pl.BlockSpec((tt, D_MODEL), row),
                   pl.BlockSpec((D_MODEL, tt), lambda i: (0, i))]
                  + [pl.BlockSpec((PEER_HEADS, PEER_NKEYS, tt), lambda i: (0, 0, i))] * 4,
        out_shape=[jax.ShapeDtypeStruct((t, D_MODEL), F32),
                   jax.ShapeDtypeStruct((D_MODEL, t), BF16)]
                  + [jax.ShapeDtypeStruct((PEER_HEADS, PEER_NKEYS, t), F32)] * 4,
        scratch_shapes=[pltpu.VMEM((2 * PEER_HEADS, tt, PEER_HALF), BF16),
                        pltpu.VMEM((3, 2 * SUBLANES, tt), F32),
                        pltpu.VMEM((N_CAND_PAD, tt), F32)],
        compiler_params=pltpu.CompilerParams(dimension_semantics=("arbitrary",), vmem_limit_bytes=VMEM_LIMIT),
        name="mid",
    )(a, hm, x2, w_out, norm2_g, w_pq, sub_keys)


def _gate_unit(key0, act_ref, need_ref, e1_ref, s2_ref, e2_ref, g_ref, row0, th):
    for q in range(PEER_PIECE // PEER_NKEYS):
        need_rows = [need_ref[h, pl.ds(key0 + q, 1), :] for h in range(PEER_HEADS)]
        e1_rows = [e1_ref[h, pl.ds(key0 + q, 1), :] for h in range(PEER_HEADS)]
        for lt in range(PEER_UNIT_TOKENS // LANES):
            la = slice(lt * LANES, (lt + 1) * LANES)
            ls = slice(th * PEER_UNIT_TOKENS + lt * LANES, th * PEER_UNIT_TOKENS + (lt + 1) * LANES)
            w = None
            for h in range(PEER_HEADS):
                wh = jnp.where(s2_ref[h, :, ls] >= need_rows[h][:, ls], e2_ref[h, :, ls], 0.0) * e1_rows[h][:, ls]
                w = wh if w is None else w + wh
            a = act_ref[q * PEER_NKEYS:(q + 1) * PEER_NKEYS, la]
            g_ref[pl.ds(pl.multiple_of(row0 + q * PEER_NKEYS, PEER_NKEYS), PEER_NKEYS), ls] = (
                a * (1.0 + lax.erf(a * np.float32(np.sqrt(0.5)))) * w).astype(BF16)


def _peer_kernel(xnt_ref, u_ref, vt_ref, need_ref, e1_ref, s2_ref, e2_ref, x1_ref, fg_ref, y_ref,
                 ga_s, gb_s, vkeep_s, acc_s, act_s):
    k = pl.program_id(1)

    @pl.when(k == 0)
    def _():
        acc_s[...] = jnp.zeros(acc_s.shape, F32)
        gb_s[...] = jnp.zeros(gb_s.shape, BF16)
        vkeep_s[...] = jnp.zeros(vkeep_s.shape, BF16)

    per_chunk = PEER_ECHUNK // PEER_PIECE
    n_piece = 2 * per_chunk
    keys_per_piece = PEER_PIECE // PEER_NKEYS

    def act_unit(p, th):
        off = pl.multiple_of(jnp.minimum(p, n_piece - 1) * PEER_PIECE, PEER_PIECE)
        act_s[th] = jnp.dot(u_ref[pl.ds(off, PEER_PIECE), :],
                            xnt_ref[:, th * PEER_UNIT_TOKENS:(th + 1) * PEER_UNIT_TOKENS],
                            preferred_element_type=F32)

    act_unit(0, 0)
    for c in range(2):
        g_prev, g_cur = (gb_s, ga_s) if c == 0 else (ga_s, gb_s)

        def piece(r, carry, c=c, g_prev=g_prev, g_cur=g_cur):
            p = c * per_chunk + r
            rows = pl.ds(pl.multiple_of(r * PEER_PIECE, PEER_PIECE), PEER_PIECE)
            v_rows = vkeep_s[rows, :] if c == 0 else vt_ref[rows, 0:PEER_ECHUNK]
            for th in range(2):
                ts = slice(th * PEER_UNIT_TOKENS, (th + 1) * PEER_UNIT_TOKENS)
                act_unit(p + th, 1 - th)
                acc_s[rows, ts] += jnp.dot(v_rows, g_prev[:, ts], preferred_element_type=F32)
                _gate_unit(p * keys_per_piece, act_s.at[th], need_ref, e1_ref, s2_ref, e2_ref, g_cur,
                           r * PEER_PIECE, th)
            return carry

        lax.fori_loop(0, per_chunk, piece, 0)
    vkeep_s[...] = vt_ref[:, PEER_ECHUNK:2 * PEER_ECHUNK]

    @pl.when(k == pl.num_programs(1) - 1)
    def _():
        acc = acc_s[...] + jnp.dot(vkeep_s[...], gb_s[...], preferred_element_type=F32)
        x2 = x1_ref[...] + acc.T
        y_ref[...] = x2 * lax.rsqrt(jnp.mean(x2 * x2, axis=-1, keepdims=True) + NORM_EPS) * fg_ref[...]


def _peer(xnt, u_bf, vt_bf, need, e1, s2, e2, x1, final_g):
    t = x1.shape[0]
    tt = min(PEER_TOKENS, t)
    assert tt == 2 * PEER_UNIT_TOKENS
    step = 2 * PEER_ECHUNK
    step_keys = step // PEER_NKEYS
    key_rows = pl.BlockSpec((PEER_HEADS, step_keys, tt), lambda i, e: (0, e, i))
    all_keys = pl.BlockSpec((PEER_HEADS, PEER_NKEYS, tt), lambda i, e: (0, 0, i))
    return pl.pallas_call(
        _peer_kernel,
        grid=(t // tt, PEER_EXPERTS // step),
        in_specs=[pl.BlockSpec((D_MODEL, tt), lambda i, e: (0, i)),
                  pl.BlockSpec((step, D_MODEL), lambda i, e: (e, 0)),
                  pl.BlockSpec((D_MODEL, step), lambda i, e: (0, e)),
                  key_rows, key_rows, all_keys, all_keys,
                  pl.BlockSpec((tt, D_MODEL), lambda i, e: (i, 0)),
                  pl.BlockSpec((1, D_MODEL), lambda i, e: (0, 0))],
        out_specs=pl.BlockSpec((tt, D_MODEL), lambda i, e: (i, 0)),
        out_shape=jax.ShapeDtypeStruct((t, D_MODEL), F32),
        scratch_shapes=[pltpu.VMEM((PEER_ECHUNK, tt), BF16),
                        pltpu.VMEM((PEER_ECHUNK, tt), BF16),
                        pltpu.VMEM((D_MODEL, PEER_ECHUNK), BF16),
                        pltpu.VMEM((D_MODEL, tt), F32),
                        pltpu.VMEM((2, PEER_PIECE, PEER_UNIT_TOKENS), F32)],
        compiler_params=pltpu.CompilerParams(dimension_semantics=("arbitrary",) * 2, vmem_limit_bytes=VMEM_LIMIT),
        name="peer_experts",
    )(xnt, u_bf, vt_bf, need, e1, s2, e2, x1, final_g)


def _alibi_slopes(n):
    start = 2.0 ** (-8.0 / n)
    return jnp.asarray(np.array([start ** (h + 1) for h in range(n)], dtype=np.float32))


def kernel(x, norm1_g, w_in, conv_w, conv_b, b_igate, b_fgate, lam_q1, lam_k1, lam_q2, lam_k2, da_subln_g, ml_norm_g, w_out, norm2_g, w_pq, sub_keys, u_emb, v_emb, final_g):
    bsz, seq, d = x.shape
    t = bsz * seq
    x2 = x.reshape(t, d)
    l = 0
    w_main = w_in[l][:, :N_MAIN].astype(BF16)
    gate_pad = jnp.zeros((d, LANES - ML_HEADS), F32)
    w_gate = jnp.concatenate([w_in[l][:, N_MAIN:N_MAIN + ML_HEADS], gate_pad,
                              w_in[l][:, N_MAIN + ML_HEADS:], gate_pad], axis=1).astype(BF16)
    bias_pad = jnp.zeros((LANES - ML_HEADS,), F32)
    b_gate = jnp.concatenate([b_igate[l], bias_pad, b_fgate[l], bias_pad])[None, :]
    lamv = jnp.stack([lam_q1[l], lam_k1[l], lam_q2[l], lam_k2[l]])

    q0, q1, da_k, da_v, ml_q, ml_k, ml_v, ml_o, gates = _inproj(x2, norm1_g[l][None, :], w_main, w_gate, b_gate)
    seq3 = lambda z: z.reshape(bsz, seq, z.shape[-1])
    a = _diff_attention(seq3(q0), seq3(q1), seq3(da_k), seq3(da_v), _alibi_slopes(DA_HEADS), lamv,
                        da_subln_g[l][None, :])
    hm = _mlstm(seq3(ml_q), seq3(ml_k), seq3(ml_v), seq3(ml_o), seq3(gates),
                conv_w[l], conv_b[l][None, :], ml_norm_g[l][None, :])
    x1, xnt, need, e1, s2, e2 = _mid(a.reshape(t, DA_WIDTH), hm.reshape(t, ML_WIDTH), x2, w_out[l].astype(BF16),
                                     norm2_g[l][None, :], w_pq[l].astype(BF16),
                                     sub_keys[l].reshape(2 * PEER_HEADS, PEER_NKEYS, PEER_HALF).astype(BF16))
    y = _peer(xnt, u_emb[l].astype(BF16), v_emb[l].T.astype(BF16), need, e1, s2, e2, x1, final_g[None, :])
    return y.reshape(bsz, seq, d)
```

```python
import functools

import numpy as np
import jax
import jax.numpy as jnp
from jax import lax
from jax.experimental import pallas as pl
from jax.experimental.pallas import tpu as pltpu

F32 = jnp.float32
BF16 = jnp.bfloat16

D_MODEL = 1024
DA_HEADS = 4
DA_QK_DIM = 64
DA_V_DIM = 128
DA_WIDTH = 512
ML_HEADS = 4
ML_HEAD_DIM = 128
ML_WIDTH = 512
ML_CONV = 4
N_MAIN = 7 * 512
PEER_HEADS = 8
PEER_NKEYS = 128
PEER_EXPERTS = PEER_NKEYS * PEER_NKEYS
PEER_HALF = 128
PEER_TOPK = 16
NORM_EPS = 1e-6
SUBLN_EPS = 1e-5
NEG_INF = -1e30
LAM_INIT = 0.2
LANES = 128
SUBLANES = 8

IN_TOKENS = 512
DA_TQ = 256
DA_TK = 512
DA_HEADS_PER_STEP = 2
ML_CHUNK = 256
MID_TOKENS = 256
PEER_TOKENS = 512
PEER_ECHUNK = 1024
PEER_PIECE = 256
PEER_UNIT_TOKENS = 256
CAND_COUNTS = tuple(PEER_TOPK // (i + 1) for i in range(PEER_TOPK))
CAND_OFFSETS = tuple(sum(CAND_COUNTS[:i]) for i in range(PEER_TOPK))
N_CAND = sum(CAND_COUNTS)
N_CAND_PAD = -(-N_CAND // SUBLANES) * SUBLANES
VMEM_LIMIT = 56 * 1024 * 1024


def _nt_dot(a, b):
    return lax.dot_general(a, b, (((1,), (1,)), ((), ())), preferred_element_type=F32)


def _inproj_kernel(x_ref, g_ref, w_ref, wg_ref, bg_ref,
                   q0_ref, q1_ref, dak_ref, dav_ref, mlq_ref, mlk_ref, mlv_ref, mlo_ref, gate_ref):
    x = x_ref[...]
    h = x * lax.rsqrt(jnp.mean(x * x, axis=-1, keepdims=True) + NORM_EPS) * g_ref[...]
    hb = h.astype(BF16)

    def mm(group):
        return jnp.dot(hb, w_ref[:, group * 512:(group + 1) * 512], preferred_element_type=F32)

    q = mm(0) * (DA_QK_DIM ** -0.5)
    lane = lax.broadcasted_iota(jnp.int32, q.shape, 1)
    first = (lane % (2 * DA_QK_DIM)) < DA_QK_DIM
    q0_ref[...] = jnp.where(first, q, 0.0).astype(BF16)
    q1_ref[...] = jnp.where(first, 0.0, q).astype(BF16)
    dak_ref[...] = mm(1).astype(BF16)
    dav_ref[...] = mm(2).astype(BF16)
    mlq_ref[...] = mm(3)
    mlk_ref[...] = mm(4)
    mlv_ref[...] = mm(5).astype(BF16)
    mlo_ref[...] = mm(6)
    gate_ref[...] = jnp.dot(hb, wg_ref[...], preferred_element_type=F32) + bg_ref[...]


def _inproj(x2, norm1_g, w_main, w_gate, b_gate):
    t = x2.shape[0]
    tt = min(IN_TOKENS, t)
    row = lambda i: (i, 0)
    fixed = lambda i: (0, 0)
    wide = lambda dt: jax.ShapeDtypeStruct((t, 512), dt)
    return pl.pallas_call(
        _inproj_kernel,
        grid=(t // tt,),
        in_specs=[pl.BlockSpec((tt, D_MODEL), row),
                  pl.BlockSpec((1, D_MODEL), fixed),
                  pl.BlockSpec((D_MODEL, N_MAIN), fixed),
                  pl.BlockSpec((D_MODEL, 2 * LANES), fixed),
                  pl.BlockSpec((1, 2 * LANES), fixed)],
        out_specs=[pl.BlockSpec((tt, 512), row)] * 8 + [pl.BlockSpec((tt, 2 * LANES), row)],
        out_shape=[wide(BF16), wide(BF16), wide(BF16), wide(BF16), wide(F32), wide(F32), wide(BF16), wide(F32),
                   jax.ShapeDtypeStruct((t, 2 * LANES), F32)],
        compiler_params=pltpu.CompilerParams(dimension_semantics=("arbitrary",), vmem_limit_bytes=VMEM_LIMIT),
        name="inproj",
    )(x2, norm1_g, w_main, w_gate, b_gate)


def _da_kernel(slope_ref, lamv_ref, g_ref, q0_ref, q1_ref, k_ref, v_ref, o_ref, m_s, acc_s, s_s):
    hp = pl.program_id(1)
    i = pl.program_id(2)
    tq = q0_ref.shape[0]
    m_s[...] = jnp.full(m_s.shape, NEG_INF, F32)
    acc_s[...] = jnp.zeros(acc_s.shape, F32)
    one_col = jnp.where(lax.broadcasted_iota(jnp.int32, (DA_TK, LANES), 1) == 0, 1.0, 0.0).astype(BF16)

    def scores(j, slot):
        kstart = pl.multiple_of(j * DA_TK, DA_TK)
        krel = (kstart - i * tq + lax.broadcasted_iota(jnp.int32, (1, DA_TK), 1)).astype(F32)
        for hh in range(DA_HEADS_PER_STEP):
            hs = slice(hh * LANES, (hh + 1) * LANES)
            q2 = jnp.concatenate([q0_ref[:, hs], q1_ref[:, hs]], axis=0)
            s_s[slot, hh] = (_nt_dot(q2, k_ref[pl.ds(kstart, DA_TK), hs])
                             + slope_ref[hp * DA_HEADS_PER_STEP + hh] * krel)

    def softmax_pv(j, slot, masked):
        kstart = pl.multiple_of(j * DA_TK, DA_TK)
        for hh in range(DA_HEADS_PER_STEP):
            hs = slice(hh * LANES, (hh + 1) * LANES)
            v_ext = jnp.concatenate([v_ref[pl.ds(kstart, DA_TK), hs], one_col], axis=1)
            s = s_s[slot, hh]
            if masked:
                row = lax.broadcasted_iota(jnp.int32, s.shape, 0)
                col = lax.broadcasted_iota(jnp.int32, s.shape, 1)
                rq = jnp.where(row >= tq, row - tq, row)
                s = jnp.where(col + (kstart - i * tq) <= rq, s, NEG_INF)
            m_prev = m_s[hh]
            m_new = jnp.maximum(m_prev, jnp.max(s, axis=-1, keepdims=True))
            alpha = jnp.exp(m_prev - m_new)
            p = jnp.exp(s - m_new).astype(BF16)
            acc_s[hh] = alpha * acc_s[hh] + jnp.dot(p, v_ext, preferred_element_type=F32)
            m_s[hh] = m_new

    last = (i * tq) // DA_TK

    def block(j, slot):
        @pl.when(j < last)
        def _():
            scores(j + 1, 1 - slot)
            softmax_pv(j, slot, False)

        @pl.when(j == last)
        def _():
            softmax_pv(j, slot, True)

    def pair(jj, carry):
        block(2 * jj, 0)
        block(2 * jj + 1, 1)
        return carry

    scores(0, 0)
    lax.fori_loop(0, last // 2 + 1, pair, 0)

    lamv = lamv_ref[...]
    lam = (jnp.exp(jnp.sum(lamv[0:1] * lamv[1:2], axis=-1, keepdims=True))
           - jnp.exp(jnp.sum(lamv[2:3] * lamv[3:4], axis=-1, keepdims=True)) + LAM_INIT)
    for hh in range(DA_HEADS_PER_STEP):
        acc = acc_s[hh]
        o = acc[:, 0:DA_V_DIM] / acc[:, DA_V_DIM:DA_V_DIM + 1]
        a = o[:tq] - lam * o[tq:]
        y = a * lax.rsqrt(jnp.mean(a * a, axis=-1, keepdims=True) + SUBLN_EPS) * g_ref[...]
        o_ref[:, hh * LANES:(hh + 1) * LANES] = (y * (1.0 - LAM_INIT)).astype(BF16)


def _diff_attention(q0, q1, k, v, slopes, lamv, subln_g):
    b, s, _ = q0.shape
    assert DA_TK % DA_TQ == 0 and s % DA_TK == 0
    hw = DA_HEADS_PER_STEP * LANES
    qspec = pl.BlockSpec((None, DA_TQ, hw), lambda bi, hi, qi: (bi, qi, hi))
    kvspec = pl.BlockSpec((None, s, hw), lambda bi, hi, qi: (bi, 0, hi))
    return pl.pallas_call(
        _da_kernel,
        grid=(b, DA_HEADS // DA_HEADS_PER_STEP, s // DA_TQ),
        in_specs=[pl.BlockSpec(memory_space=pltpu.SMEM),
                  pl.BlockSpec((4, DA_QK_DIM), lambda bi, hi, qi: (0, 0)),
                  pl.BlockSpec((1, DA_V_DIM), lambda bi, hi, qi: (0, 0)),
                  qspec, qspec, kvspec, kvspec],
        out_specs=qspec,
        out_shape=jax.ShapeDtypeStruct((b, s, DA_WIDTH), BF16),
        scratch_shapes=[pltpu.VMEM((DA_HEADS_PER_STEP, 2 * DA_TQ, 1), F32),
                        pltpu.VMEM((DA_HEADS_PER_STEP, 2 * DA_TQ, 2 * DA_V_DIM), F32),
                        pltpu.VMEM((2, DA_HEADS_PER_STEP, 2 * DA_TQ, DA_TK), F32)],
        compiler_params=pltpu.CompilerParams(dimension_semantics=("arbitrary",) * 3, vmem_limit_bytes=VMEM_LIMIT),
        name="diff_attention",
    )(slopes, lamv, subln_g, q0, q1, k, v)


def _mlstm_kernel(tril_ref, cw_ref, cb_ref, ng_ref, q_ref, k_ref, v_ref, og_ref, gate_ref, out_ref,
                  xbuf, cn_s, m_s):
    c = pl.program_id(1)
    L = q_ref.shape[0]

    @pl.when(c == 0)
    def _():
        xbuf[0:SUBLANES, :] = jnp.zeros((SUBLANES, 2 * ML_WIDTH), F32)
        cn_s[...] = jnp.zeros(cn_s.shape, F32)
        m_s[...] = jnp.full(m_s.shape, NEG_INF, F32)

    @pl.when(c > 0)
    def _():
        xbuf[0:SUBLANES, :] = xbuf[L:L + SUBLANES, :]

    xbuf[SUBLANES:SUBLANES + L, 0:ML_WIDTH] = q_ref[...]
    xbuf[SUBLANES:SUBLANES + L, ML_WIDTH:2 * ML_WIDTH] = k_ref[...]
    y = cb_ref[...] + cw_ref[0:1, :] * xbuf[SUBLANES - 3:SUBLANES - 3 + L, :]
    for j in range(1, ML_CONV):
        y = y + cw_ref[j:j + 1, :] * xbuf[SUBLANES - 3 + j:SUBLANES - 3 + j + L, :]
    qk = y * jax.nn.sigmoid(y)

    gates = gate_ref[...]
    i_log = gates[:, 0:LANES]
    f_log = jax.nn.log_sigmoid(gates[:, LANES:2 * LANES])
    b_all = jnp.dot(tril_ref[...], f_log, preferred_element_type=F32,
                    precision=lax.Precision.HIGHEST)
    r_all = i_log - b_all
    r_all_t = r_all.T
    row = lax.broadcasted_iota(jnp.int32, (L, L), 0)
    col = lax.broadcasted_iota(jnp.int32, (L, L), 1)
    tri = col <= row
    one_col = jnp.where(lax.broadcasted_iota(jnp.int32, (L, LANES), 1) == 0, 1.0, 0.0).astype(BF16)

    for h in range(ML_HEADS):
        hs = slice(h * ML_HEAD_DIM, (h + 1) * ML_HEAD_DIM)
        b_col = b_all[:, h:h + 1]
        r_col = r_all[:, h:h + 1]
        r_row = r_all_t[h:h + 1, :]
        m_prev = m_s[h:h + 1, 0:1]
        d = jnp.where(tri, b_col + r_row, NEG_INF)
        inter = b_col + m_prev
        mt = jnp.maximum(inter, jnp.max(d, axis=-1, keepdims=True))
        w_intra = jnp.exp(d - mt)
        w_inter = jnp.exp(inter - mt)
        qb = qk[:, hs].astype(BF16)
        kf = qk[:, ML_WIDTH + h * ML_HEAD_DIM:ML_WIDTH + (h + 1) * ML_HEAD_DIM] * (ML_HEAD_DIM ** -0.5)
        s = _nt_dot(qb, kf.astype(BF16)) * w_intra
        v_ext = jnp.concatenate([v_ref[:, hs], one_col], axis=1)
        cn = cn_s[h]
        num_ext = (w_inter * jnp.dot(qb, cn.astype(BF16), preferred_element_type=F32)
                   + jnp.dot(s.astype(BF16), v_ext, preferred_element_type=F32))
        num = num_ext[:, 0:ML_HEAD_DIM]
        den = num_ext[:, ML_HEAD_DIM:ML_HEAD_DIM + 1]
        hh = num / jnp.maximum(jnp.abs(den), jnp.exp(-mt))
        m_new = mt[L - 1:L, :]
        b_last = b_col[L - 1:L, :]
        ws = jnp.exp(b_last + r_col - m_new)
        decay = jnp.exp(b_last + m_prev - m_new)
        kw_t = (kf * ws).T.astype(BF16)
        cn_s[h] = decay * cn + jnp.dot(kw_t, v_ext, preferred_element_type=F32)
        m_s[h:h + 1, :] = jnp.broadcast_to(m_new, (1, LANES))
        hn = hh * lax.rsqrt(jnp.mean(hh * hh, axis=-1, keepdims=True) + NORM_EPS) * ng_ref[:, hs]
        out_ref[:, hs] = (hn * jax.nn.sigmoid(og_ref[:, hs])).astype(BF16)


def _mlstm(ml_q, ml_k, ml_v, ml_o, gates, conv_w, conv_b, norm_g):
    b, s, _ = ml_q.shape
    L = min(ML_CHUNK, s)
    tril = jnp.tril(jnp.ones((L, L), F32))
    fixed = lambda bi, ci: (0, 0)
    seq = lambda w: pl.BlockSpec((None, L, w), lambda bi, ci: (bi, ci, 0))
    return pl.pallas_call(
        _mlstm_kernel,
        grid=(b, s // L),
        in_specs=[pl.BlockSpec((L, L), fixed),
                  pl.BlockSpec((ML_CONV, 2 * ML_WIDTH), fixed),
                  pl.BlockSpec((1, 2 * ML_WIDTH), fixed),
                  pl.BlockSpec((1, ML_WIDTH), fixed),
                  seq(ML_WIDTH), seq(ML_WIDTH), seq(ML_WIDTH), seq(ML_WIDTH), seq(2 * LANES)],
        out_specs=seq(ML_WIDTH),
        out_shape=jax.ShapeDtypeStruct((b, s, ML_WIDTH), BF16),
        scratch_shapes=[pltpu.VMEM((L + 2 * SUBLANES, 2 * ML_WIDTH), F32),
                        pltpu.VMEM((ML_HEADS, ML_HEAD_DIM, 2 * ML_HEAD_DIM), F32),
                        pltpu.VMEM((SUBLANES, LANES), F32)],
        compiler_params=pltpu.CompilerParams(dimension_semantics=("arbitrary",) * 2, vmem_limit_bytes=VMEM_LIMIT),
        name="mlstm",
    )(tril, conv_w, conv_b, norm_g, ml_q, ml_k, ml_v, ml_o, gates)


def _extract_top_exact(s, n, out_ref, lead):
    rows = s.shape[0]
    iota = lax.broadcasted_iota(jnp.int32, s.shape, 0)
    for k in range(n):
        mx = jnp.max(s, axis=0, keepdims=True)
        out_ref[lead, k:k + 1, :] = mx
        first = jnp.min(jnp.where(s == mx, iota, rows), axis=0, keepdims=True)
        s = jnp.where(iota == first, -jnp.inf, s)
    return s


def _extract_top_fast(s, n, out_ref, lead, n_pad=0):
    for k in range(n):
        mx = jnp.max(s, axis=0, keepdims=True)
        out_ref[lead, k:k + 1, :] = mx
        s = jnp.where(s == mx, -jnp.inf, s)
    removed = jnp.sum(jnp.where(s == -jnp.inf, 1.0, 0.0), axis=0, keepdims=True)
    return s, removed - float(n + n_pad)


def _mid_kernel(a_ref, hm_ref, x_ref, wout_ref, g2_ref, wpq_ref, sk_ref,
                x1_ref, xnt_ref, need_ref, e1_ref, s2_ref, e2_ref, q_s, vals_s, cand_s):
    x1 = (x_ref[...]
          + jnp.dot(a_ref[...], wout_ref[0:DA_WIDTH, :], preferred_element_type=F32)
          + jnp.dot(hm_ref[...], wout_ref[DA_WIDTH:DA_WIDTH + ML_WIDTH, :], preferred_element_type=F32))
    x1_ref[...] = x1
    xn = x1 * lax.rsqrt(jnp.mean(x1 * x1, axis=-1, keepdims=True) + NORM_EPS) * g2_ref[...]
    xnt_ref[...] = xn.T.astype(BF16)
    q = jnp.dot(xn.astype(BF16), wpq_ref[...], preferred_element_type=F32).astype(BF16)
    for hc in range(2 * PEER_HEADS):
        q_s[hc] = q[:, hc * PEER_HALF:(hc + 1) * PEER_HALF]
    tt = x1.shape[0]
    cand_s[N_CAND:N_CAND_PAD, :] = jnp.full((N_CAND_PAD - N_CAND, tt), -jnp.inf, F32)

    def head(h, excess, exact):
        s1 = _nt_dot(sk_ref[2 * h], q_s[2 * h])
        s2 = _nt_dot(sk_ref[2 * h + 1], q_s[2 * h + 1])

        def extract(s, lead, n_pad=0):
            if exact:
                return _extract_top_exact(s, PEER_TOPK, vals_s, lead), excess
            left, over = _extract_top_fast(s, PEER_TOPK, vals_s, lead, n_pad)
            return left, jnp.maximum(excess, over)

        _, excess = extract(s1, 0)
        _, excess = extract(s2, 1)
        for i, cnt in enumerate(CAND_COUNTS):
            off = CAND_OFFSETS[i]
            cand_s[off:off + cnt, :] = vals_s[0, i:i + 1, :] + vals_s[1, 0:cnt, :]
        left, excess = extract(cand_s[...], 2, N_CAND_PAD - N_CAND)
        cand_s[...] = left
        top = vals_s[2, 0:PEER_TOPK, :]
        t0 = top[0:1, :]
        lse = t0 + jnp.log(jnp.sum(jnp.exp(top - t0), axis=0, keepdims=True))
        need = jnp.full(s1.shape, jnp.inf, F32)
        for i in reversed(range(PEER_TOPK)):
            cnt, off = CAND_COUNTS[i], CAND_OFFSETS[i]
            cut = jnp.min(jnp.where(cand_s[off:off + cnt, :] == -jnp.inf, vals_s[1, 0:cnt, :], jnp.inf),
                          axis=0, keepdims=True)
            need = jnp.where(s1 >= vals_s[0, i:i + 1, :], cut, need)
        c1 = vals_s[0, 0:1, :]
        need_ref[h] = need
        e1_ref[h] = jnp.exp(s1 - c1)
        s2_ref[h] = s2
        e2_ref[h] = 0.5 * jnp.exp(s2 - (lse - c1))
        return excess

    excess = lax.fori_loop(0, PEER_HEADS, functools.partial(head, exact=False), jnp.zeros((1, tt), F32))

    @pl.when(jnp.max(excess) > 0.0)
    def _():
        lax.fori_loop(0, PEER_HEADS, functools.partial(head, exact=True), jnp.zeros((1, tt), F32))


def _mid(a, hm, x2, w_out, norm2_g, w_pq, sub_keys):
    t = x2.shape[0]
    tt = min(MID_TOKENS, t)
    row = lambda i: (i, 0)
    fixed = lambda i: (0, 0)
    return pl.pallas_call(
        _mid_kernel,
        grid=(t // tt,),
        in_specs=[pl.BlockSpec((tt, DA_WIDTH), row),
                  pl.BlockSpec((tt, ML_WIDTH), row),
                  pl.BlockSpec((tt, D_MODEL), row),
                  pl.BlockSpec((D_MODEL, D_MODEL), fixed),
                  pl.BlockSpec((1, D_MODEL), fixed),
                  pl.BlockSpec((D_MODEL, 2 * PEER_HEADS * PEER_HALF), fixed),
                  pl.BlockSpec((2 * PEER_HEADS, PEER_NKEYS, PEER_HALF), lambda i: (0, 0, 0))],
        out_specs=[pl.BlockSpec((tt, D_MODEL), row),
                   pl.BlockSpec((None, D_MODEL, tt), lambda i: (i, 0, 0))]
                  + [pl.BlockSpec((None, PEER_HEADS, PEER_NKEYS, tt), lambda i: (i, 0, 0, 0))] * 4,
        out_shape=[jax.ShapeDtypeStruct((t, D_MODEL), F32),
                   jax.ShapeDtypeStruct((t // tt, D_MODEL, tt), BF16)]
                  + [jax.ShapeDtypeStruct((t // tt, PEER_HEADS, PEER_NKEYS, tt), F32)] * 4,
        scratch_shapes=[pltpu.VMEM((2 * PEER_HEADS, tt, PEER_HALF), BF16),
                        pltpu.VMEM((3, 2 * SUBLANES, tt), F32),
                        pltpu.VMEM((N_CAND_PAD, tt), F32)],
        compiler_params=pltpu.CompilerParams(dimension_semantics=("arbitrary",), vmem_limit_bytes=VMEM_LIMIT),
        name="mid",
    )(a, hm, x2, w_out, norm2_g, w_pq, sub_keys)


def _gate_unit(key0, act_ref, need_ref, e1_ref, s2_ref, e2_ref, g_ref, row0, th):
    for q in range(PEER_PIECE // PEER_NKEYS):
        need_rows = [need_ref[th, h, pl.ds(key0 + q, 1), :] for h in range(PEER_HEADS)]
        e1_rows = [e1_ref[th, h, pl.ds(key0 + q, 1), :] for h in range(PEER_HEADS)]
        for lt in range(PEER_UNIT_TOKENS // LANES):
            la = slice(lt * LANES, (lt + 1) * LANES)
            ls = slice(th * PEER_UNIT_TOKENS + lt * LANES, th * PEER_UNIT_TOKENS + (lt + 1) * LANES)
            w = None
            for h in range(PEER_HEADS):
                wh = jnp.where(s2_ref[th, h, :, la] >= need_rows[h][:, la], e2_ref[th, h, :, la], 0.0) * e1_rows[h][:, la]
                w = wh if w is None else w + wh
            a = act_ref[q * PEER_NKEYS:(q + 1) * PEER_NKEYS, la]
            g_ref[pl.ds(pl.multiple_of(row0 + q * PEER_NKEYS, PEER_NKEYS), PEER_NKEYS), ls] = (
                a * (1.0 + lax.erf(a * np.float32(np.sqrt(0.5)))) * w).astype(BF16)


def _peer_kernel(xnt_ref, u_ref, vt_ref, need_ref, e1_ref, s2_ref, e2_ref, x1_ref, fg_ref, y_ref,
                 ga_s, gb_s, vkeep_s, acc_s, act_s):
    k = pl.program_id(1)

    @pl.when(k == 0)
    def _():
        acc_s[...] = jnp.zeros(acc_s.shape, F32)
        gb_s[...] = jnp.zeros(gb_s.shape, BF16)
        vkeep_s[...] = jnp.zeros(vkeep_s.shape, BF16)

    per_chunk = PEER_ECHUNK // PEER_PIECE
    n_piece = 2 * per_chunk
    keys_per_piece = PEER_PIECE // PEER_NKEYS

    def act_unit(p, th):
        off = pl.multiple_of(jnp.minimum(p, n_piece - 1) * PEER_PIECE, PEER_PIECE)
        act_s[th] = jnp.dot(u_ref[pl.ds(off, PEER_PIECE), :], xnt_ref[th], preferred_element_type=F32)

    act_unit(0, 0)
    for c in range(2):
        g_prev, g_cur = (gb_s, ga_s) if c == 0 else (ga_s, gb_s)

        def piece(r, carry, c=c, g_prev=g_prev, g_cur=g_cur):
            p = c * per_chunk + r
            rows = pl.ds(pl.multiple_of(r * PEER_PIECE, PEER_PIECE), PEER_PIECE)
            v_rows = vkeep_s[rows, :] if c == 0 else vt_ref[rows, 0:PEER_ECHUNK]
            for th in range(2):
                ts = slice(th * PEER_UNIT_TOKENS, (th + 1) * PEER_UNIT_TOKENS)
                act_unit(p + th, 1 - th)
                acc_s[rows, ts] += jnp.dot(v_rows, g_prev[:, ts], preferred_element_type=F32)
                _gate_unit(p * keys_per_piece, act_s.at[th], need_ref, e1_ref, s2_ref, e2_ref, g_cur,
                           r * PEER_PIECE, th)
            return carry

        lax.fori_loop(0, per_chunk, piece, 0)
    vkeep_s[...] = vt_ref[:, PEER_ECHUNK:2 * PEER_ECHUNK]

    @pl.when(k == pl.num_programs(1) - 1)
    def _():
        acc = acc_s[...] + jnp.dot(vkeep_s[...], gb_s[...], preferred_element_type=F32)
        x2 = x1_ref[...] + acc.T
        y_ref[...] = x2 * lax.rsqrt(jnp.mean(x2 * x2, axis=-1, keepdims=True) + NORM_EPS) * fg_ref[...]


def _peer(xnt, u_bf, vt_bf, need, e1, s2, e2, x1, final_g):
    t = x1.shape[0]
    tt = min(PEER_TOKENS, t)
    assert tt == 2 * PEER_UNIT_TOKENS and need.shape[-1] == PEER_UNIT_TOKENS
    step = 2 * PEER_ECHUNK
    step_keys = step // PEER_NKEYS
    halves = tt // PEER_UNIT_TOKENS
    key_rows = pl.BlockSpec((halves, PEER_HEADS, step_keys, PEER_UNIT_TOKENS), lambda i, e: (i, 0, e, 0))
    all_keys = pl.BlockSpec((halves, PEER_HEADS, PEER_NKEYS, PEER_UNIT_TOKENS), lambda i, e: (i, 0, 0, 0))
    return pl.pallas_call(
        _peer_kernel,
        grid=(t // tt, PEER_EXPERTS // step),
        in_specs=[pl.BlockSpec((halves, D_MODEL, PEER_UNIT_TOKENS), lambda i, e: (i, 0, 0)),
                  pl.BlockSpec((step, D_MODEL), lambda i, e: (e, 0)),
                  pl.BlockSpec((D_MODEL, step), lambda i, e: (0, e)),
                  key_rows, key_rows, all_keys, all_keys,
                  pl.BlockSpec((tt, D_MODEL), lambda i, e: (i, 0)),
                  pl.BlockSpec((1, D_MODEL), lambda i, e: (0, 0))],
        out_specs=pl.BlockSpec((tt, D_MODEL), lambda i, e: (i, 0)),
        out_shape=jax.ShapeDtypeStruct((t, D_MODEL), F32),
        scratch_shapes=[pltpu.VMEM((PEER_ECHUNK, tt), BF16),
                        pltpu.VMEM((PEER_ECHUNK, tt), BF16),
                        pltpu.VMEM((D_MODEL, PEER_ECHUNK), BF16),
                        pltpu.VMEM((D_MODEL, tt), F32),
                        pltpu.VMEM((2, PEER_PIECE, PEER_UNIT_TOKENS), F32)],
        compiler_params=pltpu.CompilerParams(dimension_semantics=("arbitrary",) * 2, vmem_limit_bytes=VMEM_LIMIT),
        name="peer_experts",
    )(xnt, u_bf, vt_bf, need, e1, s2, e2, x1, final_g)


def _alibi_slopes(n):
    start = 2.0 ** (-8.0 / n)
    return jnp.asarray(np.array([start ** (h + 1) for h in range(n)], dtype=np.float32))


def kernel(x, norm1_g, w_in, conv_w, conv_b, b_igate, b_fgate, lam_q1, lam_k1, lam_q2, lam_k2, da_subln_g, ml_norm_g, w_out, norm2_g, w_pq, sub_keys, u_emb, v_emb, final_g):
    bsz, seq, d = x.shape
    t = bsz * seq
    x2 = x.reshape(t, d)
    l = 0
    w_main = w_in[l][:, :N_MAIN].astype(BF16)
    gate_pad = jnp.zeros((d, LANES - ML_HEADS), F32)
    w_gate = jnp.concatenate([w_in[l][:, N_MAIN:N_MAIN + ML_HEADS], gate_pad,
                              w_in[l][:, N_MAIN + ML_HEADS:], gate_pad], axis=1).astype(BF16)
    bias_pad = jnp.zeros((LANES - ML_HEADS,), F32)
    b_gate = jnp.concatenate([b_igate[l], bias_pad, b_fgate[l], bias_pad])[None, :]
    lamv = jnp.stack([lam_q1[l], lam_k1[l], lam_q2[l], lam_k2[l]])

    q0, q1, da_k, da_v, ml_q, ml_k, ml_v, ml_o, gates = _inproj(x2, norm1_g[l][None, :], w_main, w_gate, b_gate)
    seq3 = lambda z: z.reshape(bsz, seq, z.shape[-1])
    a = _diff_attention(seq3(q0), seq3(q1), seq3(da_k), seq3(da_v), _alibi_slopes(DA_HEADS), lamv,
                        da_subln_g[l][None, :])
    hm = _mlstm(seq3(ml_q), seq3(ml_k), seq3(ml_v), seq3(ml_o), seq3(gates),
                conv_w[l], conv_b[l][None, :], ml_norm_g[l][None, :])
    x1, xnt, need, e1, s2, e2 = _mid(a.reshape(t, DA_WIDTH), hm.reshape(t, ML_WIDTH), x2, w_out[l].astype(BF16),
                                     norm2_g[l][None, :], w_pq[l].astype(BF16),
                                     sub_keys[l].reshape(2 * PEER_HEADS, PEER_NKEYS, PEER_HALF).astype(BF16))
    y = _peer(xnt, u_emb[l].astype(BF16), v_emb[l].T.astype(BF16), need, e1, s2, e2, x1, final_g[None, :])
    return y.reshape(bsz, seq, d)
```

```python
import functools

import numpy as np
import jax
import jax.numpy as jnp
from jax import lax
from jax.experimental import pallas as pl
from jax.experimental.pallas import tpu as pltpu

F32 = jnp.float32
BF16 = jnp.bfloat16

D_MODEL = 1024
DA_HEADS = 4
DA_QK_DIM = 64
DA_V_DIM = 128
DA_WIDTH = 512
ML_HEADS = 4
ML_HEAD_DIM = 128
ML_WIDTH = 512
ML_CONV = 4
N_MAIN = 7 * 512
PEER_HEADS = 8
PEER_NKEYS = 128
PEER_EXPERTS = PEER_NKEYS * PEER_NKEYS
PEER_HALF = 128
PEER_TOPK = 16
NORM_EPS = 1e-6
SUBLN_EPS = 1e-5
NEG_INF = -1e30
LAM_INIT = 0.2
LANES = 128
SUBLANES = 8

IN_TOKENS = 512
DA_TQ = 256
DA_TK = 512
DA_HEADS_PER_STEP = 2
ML_CHUNK = 256
MID_TOKENS = 512
PEER_TOKENS = 512
PEER_ECHUNK = 1024
PEER_PIECE = 512
PEER_UNIT_TOKENS = 256
CAND_COUNTS = tuple(PEER_TOPK // (i + 1) for i in range(PEER_TOPK))
CAND_OFFSETS = tuple(sum(CAND_COUNTS[:i]) for i in range(PEER_TOPK))
N_CAND = sum(CAND_COUNTS)
N_CAND_PAD = -(-N_CAND // SUBLANES) * SUBLANES
VMEM_LIMIT = 56 * 1024 * 1024


def _nt_dot(a, b):
    return lax.dot_general(a, b, (((1,), (1,)), ((), ())), preferred_element_type=F32)


def _inproj_kernel(x_ref, g_ref, w_ref, wg_ref, bg_ref,
                   q0_ref, q1_ref, dak_ref, dav_ref, mlq_ref, mlk_ref, mlv_ref, mlo_ref, gate_ref):
    x = x_ref[...]
    h = x * lax.rsqrt(jnp.mean(x * x, axis=-1, keepdims=True) + NORM_EPS) * g_ref[...]
    hb = h.astype(BF16)

    def mm(group):
        return jnp.dot(hb, w_ref[:, group * 512:(group + 1) * 512], preferred_element_type=F32)

    q = mm(0) * (DA_QK_DIM ** -0.5)
    lane = lax.broadcasted_iota(jnp.int32, q.shape, 1)
    first = (lane % (2 * DA_QK_DIM)) < DA_QK_DIM
    q0_ref[...] = jnp.where(first, q, 0.0).astype(BF16)
    q1_ref[...] = jnp.where(first, 0.0, q).astype(BF16)
    dak_ref[...] = mm(1).astype(BF16)
    dav_ref[...] = mm(2).astype(BF16)
    mlq_ref[...] = mm(3)
    mlk_ref[...] = mm(4)
    mlv_ref[...] = mm(5).astype(BF16)
    mlo_ref[...] = mm(6)
    gate_ref[...] = jnp.dot(hb, wg_ref[...], preferred_element_type=F32) + bg_ref[...]


def _inproj(x2, norm1_g, w_main, w_gate, b_gate):
    t = x2.shape[0]
    tt = min(IN_TOKENS, t)
    row = lambda i: (i, 0)
    fixed = lambda i: (0, 0)
    wide = lambda dt: jax.ShapeDtypeStruct((t, 512), dt)
    return pl.pallas_call(
        _inproj_kernel,
        grid=(t // tt,),
        in_specs=[pl.BlockSpec((tt, D_MODEL), row),
                  pl.BlockSpec((1, D_MODEL), fixed),
                  pl.BlockSpec((D_MODEL, N_MAIN), fixed),
                  pl.BlockSpec((D_MODEL, 2 * LANES), fixed),
                  pl.BlockSpec((1, 2 * LANES), fixed)],
        out_specs=[pl.BlockSpec((tt, 512), row)] * 8 + [pl.BlockSpec((tt, 2 * LANES), row)],
        out_shape=[wide(BF16), wide(BF16), wide(BF16), wide(BF16), wide(F32), wide(F32), wide(BF16), wide(F32),
                   jax.ShapeDtypeStruct((t, 2 * LANES), F32)],
        compiler_params=pltpu.CompilerParams(dimension_semantics=("arbitrary",), vmem_limit_bytes=VMEM_LIMIT),
        name="inproj",
    )(x2, norm1_g, w_main, w_gate, b_gate)


def _da_kernel(slope_ref, lamv_ref, g_ref, q0_ref, q1_ref, k_ref, v_ref, o_ref, m_s, acc_s, s_s):
    hp = pl.program_id(1)
    i = pl.program_id(2)
    tq = q0_ref.shape[0]
    m_s[...] = jnp.full(m_s.shape, NEG_INF, F32)
    acc_s[...] = jnp.zeros(acc_s.shape, F32)
    one_col = jnp.where(lax.broadcasted_iota(jnp.int32, (DA_TK, LANES), 1) == 0, 1.0, 0.0).astype(BF16)

    def scores(j, slot):
        kstart = pl.multiple_of(j * DA_TK, DA_TK)
        krel = (kstart - i * tq + lax.broadcasted_iota(jnp.int32, (1, DA_TK), 1)).astype(F32)
        for hh in range(DA_HEADS_PER_STEP):
            hs = slice(hh * LANES, (hh + 1) * LANES)
            q2 = jnp.concatenate([q0_ref[:, hs], q1_ref[:, hs]], axis=0)
            s_s[slot, hh] = (_nt_dot(q2, k_ref[pl.ds(kstart, DA_TK), hs])
                             + slope_ref[hp * DA_HEADS_PER_STEP + hh] * krel)

    def softmax_pv(j, slot, masked):
        kstart = pl.multiple_of(j * DA_TK, DA_TK)
        for hh in range(DA_HEADS_PER_STEP):
            hs = slice(hh * LANES, (hh + 1) * LANES)
            v_ext = jnp.concatenate([v_ref[pl.ds(kstart, DA_TK), hs], one_col], axis=1)
            s = s_s[slot, hh]
            if masked:
                row = lax.broadcasted_iota(jnp.int32, s.shape, 0)
                col = lax.broadcasted_iota(jnp.int32, s.shape, 1)
                rq = jnp.where(row >= tq, row - tq, row)
                s = jnp.where(col + (kstart - i * tq) <= rq, s, NEG_INF)
            m_prev = m_s[hh]
            m_new = jnp.maximum(m_prev, jnp.max(s, axis=-1, keepdims=True))
            alpha = jnp.exp(m_prev - m_new)
            p = jnp.exp(s - m_new).astype(BF16)
            acc_s[hh] = alpha * acc_s[hh] + jnp.dot(p, v_ext, preferred_element_type=F32)
            m_s[hh] = m_new

    last = (i * tq) // DA_TK

    def block(j, slot):
        @pl.when(j < last)
        def _():
            scores(j + 1, 1 - slot)
            softmax_pv(j, slot, False)

        @pl.when(j == last)
        def _():
            softmax_pv(j, slot, True)

    def pair(jj, carry):
        block(2 * jj, 0)
        block(2 * jj + 1, 1)
        return carry

    scores(0, 0)
    lax.fori_loop(0, last // 2 + 1, pair, 0)

    lamv = lamv_ref[...]
    lam = (jnp.exp(jnp.sum(lamv[0:1] * lamv[1:2], axis=-1, keepdims=True))
           - jnp.exp(jnp.sum(lamv[2:3] * lamv[3:4], axis=-1, keepdims=True)) + LAM_INIT)
    for hh in range(DA_HEADS_PER_STEP):
        acc = acc_s[hh]
        o = acc[:, 0:DA_V_DIM] / acc[:, DA_V_DIM:DA_V_DIM + 1]
        a = o[:tq] - lam * o[tq:]
        y = a * lax.rsqrt(jnp.mean(a * a, axis=-1, keepdims=True) + SUBLN_EPS) * g_ref[...]
        o_ref[:, hh * LANES:(hh + 1) * LANES] = (y * (1.0 - LAM_INIT)).astype(BF16)


def _diff_attention(q0, q1, k, v, slopes, lamv, subln_g):
    b, s, _ = q0.shape
    assert DA_TK % DA_TQ == 0 and s % DA_TK == 0
    hw = DA_HEADS_PER_STEP * LANES
    qspec = pl.BlockSpec((None, DA_TQ, hw), lambda bi, hi, qi: (bi, qi, hi))
    kvspec = pl.BlockSpec((None, s, hw), lambda bi, hi, qi: (bi, 0, hi))
    return pl.pallas_call(
        _da_kernel,
        grid=(b, DA_HEADS // DA_HEADS_PER_STEP, s // DA_TQ),
        in_specs=[pl.BlockSpec(memory_space=pltpu.SMEM),
                  pl.BlockSpec((4, DA_QK_DIM), lambda bi, hi, qi: (0, 0)),
                  pl.BlockSpec((1, DA_V_DIM), lambda bi, hi, qi: (0, 0)),
                  qspec, qspec, kvspec, kvspec],
        out_specs=qspec,
        out_shape=jax.ShapeDtypeStruct((b, s, DA_WIDTH), BF16),
        scratch_shapes=[pltpu.VMEM((DA_HEADS_PER_STEP, 2 * DA_TQ, 1), F32),
                        pltpu.VMEM((DA_HEADS_PER_STEP, 2 * DA_TQ, 2 * DA_V_DIM), F32),
                        pltpu.VMEM((2, DA_HEADS_PER_STEP, 2 * DA_TQ, DA_TK), F32)],
        compiler_params=pltpu.CompilerParams(dimension_semantics=("arbitrary",) * 3, vmem_limit_bytes=VMEM_LIMIT),
        name="diff_attention",
    )(slopes, lamv, subln_g, q0, q1, k, v)


def _mlstm_kernel(tril_ref, cw_ref, cb_ref, ng_ref, q_ref, k_ref, v_ref, og_ref, gate_ref, out_ref,
                  xbuf, cn_s, m_s):
    c = pl.program_id(1)
    L = q_ref.shape[0]

    @pl.when(c == 0)
    def _():
        xbuf[0:SUBLANES, :] = jnp.zeros((SUBLANES, 2 * ML_WIDTH), F32)
        cn_s[...] = jnp.zeros(cn_s.shape, F32)
        m_s[...] = jnp.full(m_s.shape, NEG_INF, F32)

    @pl.when(c > 0)
    def _():
        xbuf[0:SUBLANES, :] = xbuf[L:L + SUBLANES, :]

    xbuf[SUBLANES:SUBLANES + L, 0:ML_WIDTH] = q_ref[...]
    xbuf[SUBLANES:SUBLANES + L, ML_WIDTH:2 * ML_WIDTH] = k_ref[...]
    y = cb_ref[...] + cw_ref[0:1, :] * xbuf[SUBLANES - 3:SUBLANES - 3 + L, :]
    for j in range(1, ML_CONV):
        y = y + cw_ref[j:j + 1, :] * xbuf[SUBLANES - 3 + j:SUBLANES - 3 + j + L, :]
    qk = y * jax.nn.sigmoid(y)

    gates = gate_ref[...]
    i_log = gates[:, 0:LANES]
    f_log = jax.nn.log_sigmoid(gates[:, LANES:2 * LANES])
    b_all = jnp.dot(tril_ref[...], f_log, preferred_element_type=F32,
                    precision=lax.Precision.HIGHEST)
    r_all = i_log - b_all
    r_all_t = r_all.T
    row = lax.broadcasted_iota(jnp.int32, (L, L), 0)
    col = lax.broadcasted_iota(jnp.int32, (L, L), 1)
    tri = col <= row
    one_col = jnp.where(lax.broadcasted_iota(jnp.int32, (L, LANES), 1) == 0, 1.0, 0.0).astype(BF16)

    for h in range(ML_HEADS):
        hs = slice(h * ML_HEAD_DIM, (h + 1) * ML_HEAD_DIM)
        b_col = b_all[:, h:h + 1]
        r_col = r_all[:, h:h + 1]
        r_row = r_all_t[h:h + 1, :]
        m_prev = m_s[h:h + 1, 0:1]
        d = jnp.where(tri, b_col + r_row, NEG_INF)
        inter = b_col + m_prev
        mt = jnp.maximum(inter, jnp.max(d, axis=-1, keepdims=True))
        w_intra = jnp.exp(d - mt)
        w_inter = jnp.exp(inter - mt)
        qb = qk[:, hs].astype(BF16)
        kf = qk[:, ML_WIDTH + h * ML_HEAD_DIM:ML_WIDTH + (h + 1) * ML_HEAD_DIM] * (ML_HEAD_DIM ** -0.5)
        s = _nt_dot(qb, kf.astype(BF16)) * w_intra
        v_ext = jnp.concatenate([v_ref[:, hs], one_col], axis=1)
        cn = cn_s[h]
        num_ext = (w_inter * jnp.dot(qb, cn.astype(BF16), preferred_element_type=F32)
                   + jnp.dot(s.astype(BF16), v_ext, preferred_element_type=F32))
        num = num_ext[:, 0:ML_HEAD_DIM]
        den = num_ext[:, ML_HEAD_DIM:ML_HEAD_DIM + 1]
        hh = num / jnp.maximum(jnp.abs(den), jnp.exp(-mt))
        m_new = mt[L - 1:L, :]
        b_last = b_col[L - 1:L, :]
        ws = jnp.exp(b_last + r_col - m_new)
        decay = jnp.exp(b_last + m_prev - m_new)
        kw_t = (kf * ws).T.astype(BF16)
        cn_s[h] = decay * cn + jnp.dot(kw_t, v_ext, preferred_element_type=F32)
        m_s[h:h + 1, :] = jnp.broadcast_to(m_new, (1, LANES))
        hn = hh * lax.rsqrt(jnp.mean(hh * hh, axis=-1, keepdims=True) + NORM_EPS) * ng_ref[:, hs]
        out_ref[:, hs] = (hn * jax.nn.sigmoid(og_ref[:, hs])).astype(BF16)


def _mlstm(ml_q, ml_k, ml_v, ml_o, gates, conv_w, conv_b, norm_g):
    b, s, _ = ml_q.shape
    L = min(ML_CHUNK, s)
    tril = jnp.tril(jnp.ones((L, L), F32))
    fixed = lambda bi, ci: (0, 0)
    seq = lambda w: pl.BlockSpec((None, L, w), lambda bi, ci: (bi, ci, 0))
    return pl.pallas_call(
        _mlstm_kernel,
        grid=(b, s // L),
        in_specs=[pl.BlockSpec((L, L), fixed),
                  pl.BlockSpec((ML_CONV, 2 * ML_WIDTH), fixed),
                  pl.BlockSpec((1, 2 * ML_WIDTH), fixed),
                  pl.BlockSpec((1, ML_WIDTH), fixed),
                  seq(ML_WIDTH), seq(ML_WIDTH), seq(ML_WIDTH), seq(ML_WIDTH), seq(2 * LANES)],
        out_specs=seq(ML_WIDTH),
        out_shape=jax.ShapeDtypeStruct((b, s, ML_WIDTH), BF16),
        scratch_shapes=[pltpu.VMEM((L + 2 * SUBLANES, 2 * ML_WIDTH), F32),
                        pltpu.VMEM((ML_HEADS, ML_HEAD_DIM, 2 * ML_HEAD_DIM), F32),
                        pltpu.VMEM((SUBLANES, LANES), F32)],
        compiler_params=pltpu.CompilerParams(dimension_semantics=("arbitrary",) * 2, vmem_limit_bytes=VMEM_LIMIT),
        name="mlstm",
    )(tril, conv_w, conv_b, norm_g, ml_q, ml_k, ml_v, ml_o, gates)


def _extract_top_exact(s, n, out_ref, lead):
    rows = s.shape[0]
    iota = lax.broadcasted_iota(jnp.int32, s.shape, 0)
    for k in range(n):
        mx = jnp.max(s, axis=0, keepdims=True)
        out_ref[lead, k:k + 1, :] = mx
        first = jnp.min(jnp.where(s == mx, iota, rows), axis=0, keepdims=True)
        s = jnp.where(iota == first, -jnp.inf, s)
    return s


def _extract_top_fast(s, n, out_ref, lead, n_pad=0):
    for k in range(n):
        mx = jnp.max(s, axis=0, keepdims=True)
        out_ref[lead, k:k + 1, :] = mx
        s = jnp.where(s == mx, -jnp.inf, s)
    removed = jnp.sum(jnp.where(s == -jnp.inf, 1.0, 0.0), axis=0, keepdims=True)
    return s, removed - float(n + n_pad)


def _mid_kernel(a_ref, hm_ref, x_ref, wout_ref, g2_ref, wpq_ref, sk_ref,
                x1_ref, xnt_ref, need_ref, e1_ref, s2_ref, e2_ref, q_s, vals_s, cand_s):
    x1 = (x_ref[...]
          + jnp.dot(a_ref[...], wout_ref[0:DA_WIDTH, :], preferred_element_type=F32)
          + jnp.dot(hm_ref[...], wout_ref[DA_WIDTH:DA_WIDTH + ML_WIDTH, :], preferred_element_type=F32))
    x1_ref[...] = x1
    xn = x1 * lax.rsqrt(jnp.mean(x1 * x1, axis=-1, keepdims=True) + NORM_EPS) * g2_ref[...]
    xnt = xn.T.astype(BF16)
    tiles = [slice(i * PEER_UNIT_TOKENS, (i + 1) * PEER_UNIT_TOKENS) for i in range(xnt.shape[1] // PEER_UNIT_TOKENS)]
    for i, ts in enumerate(tiles):
        xnt_ref[i] = xnt[:, ts]
    q = jnp.dot(xn.astype(BF16), wpq_ref[...], preferred_element_type=F32).astype(BF16)
    for hc in range(2 * PEER_HEADS):
        q_s[hc] = q[:, hc * PEER_HALF:(hc + 1) * PEER_HALF]
    tt = x1.shape[0]
    cand_s[N_CAND:N_CAND_PAD, :] = jnp.full((N_CAND_PAD - N_CAND, tt), -jnp.inf, F32)

    def head(h, excess, exact):
        s1 = _nt_dot(sk_ref[2 * h], q_s[2 * h])
        s2 = _nt_dot(sk_ref[2 * h + 1], q_s[2 * h + 1])

        def extract(s, lead, n_pad=0):
            if exact:
                return _extract_top_exact(s, PEER_TOPK, vals_s, lead), excess
            left, over = _extract_top_fast(s, PEER_TOPK, vals_s, lead, n_pad)
            return left, jnp.maximum(excess, over)

        _, excess = extract(s1, 0)
        _, excess = extract(s2, 1)
        for i, cnt in enumerate(CAND_COUNTS):
            off = CAND_OFFSETS[i]
            cand_s[off:off + cnt, :] = vals_s[0, i:i + 1, :] + vals_s[1, 0:cnt, :]
        left, excess = extract(cand_s[...], 2, N_CAND_PAD - N_CAND)
        cand_s[...] = left
        top = vals_s[2, 0:PEER_TOPK, :]
        t0 = top[0:1, :]
        lse = t0 + jnp.log(jnp.sum(jnp.exp(top - t0), axis=0, keepdims=True))
        need = jnp.full(s1.shape, jnp.inf, F32)
        for i in reversed(range(PEER_TOPK)):
            cnt, off = CAND_COUNTS[i], CAND_OFFSETS[i]
            cut = jnp.min(jnp.where(cand_s[off:off + cnt, :] == -jnp.inf, vals_s[1, 0:cnt, :], jnp.inf),
                          axis=0, keepdims=True)
            need = jnp.where(s1 >= vals_s[0, i:i + 1, :], cut, need)
        c1 = vals_s[0, 0:1, :]
        e1 = jnp.exp(s1 - c1)
        e2 = 0.5 * jnp.exp(s2 - (lse - c1))
        for i, ts in enumerate(tiles):
            need_ref[i, h] = need[:, ts]
            e1_ref[i, h] = e1[:, ts]
            s2_ref[i, h] = s2[:, ts]
            e2_ref[i, h] = e2[:, ts]
        return excess

    excess = lax.fori_loop(0, PEER_HEADS, functools.partial(head, exact=False), jnp.zeros((1, tt), F32))

    @pl.when(jnp.max(excess) > 0.0)
    def _():
        lax.fori_loop(0, PEER_HEADS, functools.partial(head, exact=True), jnp.zeros((1, tt), F32))


def _mid(a, hm, x2, w_out, norm2_g, w_pq, sub_keys):
    t = x2.shape[0]
    tt = min(MID_TOKENS, t)
    tile = PEER_UNIT_TOKENS
    n_tile = tt // tile
    row = lambda i: (i, 0)
    fixed = lambda i: (0, 0)
    return pl.pallas_call(
        _mid_kernel,
        grid=(t // tt,),
        in_specs=[pl.BlockSpec((tt, DA_WIDTH), row),
                  pl.BlockSpec((tt, ML_WIDTH), row),
                  pl.BlockSpec((tt, D_MODEL), row),
                  pl.BlockSpec((D_MODEL, D_MODEL), fixed),
                  pl.BlockSpec((1, D_MODEL), fixed),
                  pl.BlockSpec((D_MODEL, 2 * PEER_HEADS * PEER_HALF), fixed),
                  pl.BlockSpec((2 * PEER_HEADS, PEER_NKEYS, PEER_HALF), lambda i: (0, 0, 0))],
        out_specs=[pl.BlockSpec((tt, D_MODEL), row),
                   pl.BlockSpec((n_tile, D_MODEL, tile), lambda i: (i, 0, 0))]
                  + [pl.BlockSpec((n_tile, PEER_HEADS, PEER_NKEYS, tile), lambda i: (i, 0, 0, 0))] * 4,
        out_shape=[jax.ShapeDtypeStruct((t, D_MODEL), F32),
                   jax.ShapeDtypeStruct((t // tile, D_MODEL, tile), BF16)]
                  + [jax.ShapeDtypeStruct((t // tile, PEER_HEADS, PEER_NKEYS, tile), F32)] * 4,
        scratch_shapes=[pltpu.VMEM((2 * PEER_HEADS, tt, PEER_HALF), BF16),
                        pltpu.VMEM((3, 2 * SUBLANES, tt), F32),
                        pltpu.VMEM((N_CAND_PAD, tt), F32)],
        compiler_params=pltpu.CompilerParams(dimension_semantics=("arbitrary",), vmem_limit_bytes=VMEM_LIMIT),
        name="mid",
    )(a, hm, x2, w_out, norm2_g, w_pq, sub_keys)


def _gate_unit(key0, act_ref, need_ref, e1_ref, s2_ref, e2_ref, g_ref, row0, th):
    for q in range(PEER_PIECE // PEER_NKEYS):
        need_rows = [need_ref[th, h, pl.ds(key0 + q, 1), :] for h in range(PEER_HEADS)]
        e1_rows = [e1_ref[th, h, pl.ds(key0 + q, 1), :] for h in range(PEER_HEADS)]
        for lt in range(PEER_UNIT_TOKENS // LANES):
            la = slice(lt * LANES, (lt + 1) * LANES)
            ls = slice(th * PEER_UNIT_TOKENS + lt * LANES, th * PEER_UNIT_TOKENS + (lt + 1) * LANES)
            w = None
            for h in range(PEER_HEADS):
                wh = jnp.where(s2_ref[th, h, :, la] >= need_rows[h][:, la], e2_ref[th, h, :, la], 0.0) * e1_rows[h][:, la]
                w = wh if w is None else w + wh
            a = act_ref[q * PEER_NKEYS:(q + 1) * PEER_NKEYS, la]
            g_ref[pl.ds(pl.multiple_of(row0 + q * PEER_NKEYS, PEER_NKEYS), PEER_NKEYS), ls] = (
                a * (1.0 + lax.erf(a * np.float32(np.sqrt(0.5)))) * w).astype(BF16)


def _peer_kernel(xnt_ref, u_ref, vt_ref, need_ref, e1_ref, s2_ref, e2_ref, x1_ref, fg_ref, y_ref,
                 ga_s, gb_s, vkeep_s, acc_s, act_s):
    k = pl.program_id(1)

    @pl.when(k == 0)
    def _():
        acc_s[...] = jnp.zeros(acc_s.shape, F32)
        gb_s[...] = jnp.zeros(gb_s.shape, BF16)
        vkeep_s[...] = jnp.zeros(vkeep_s.shape, BF16)

    per_chunk = PEER_ECHUNK // PEER_PIECE
    n_piece = 2 * per_chunk
    keys_per_piece = PEER_PIECE // PEER_NKEYS

    def act_unit(p, th):
        off = pl.multiple_of(jnp.minimum(p, n_piece - 1) * PEER_PIECE, PEER_PIECE)
        act_s[th] = jnp.dot(u_ref[pl.ds(off, PEER_PIECE), :], xnt_ref[th], preferred_element_type=F32)

    act_unit(0, 0)
    for c in range(2):
        g_prev, g_cur = (gb_s, ga_s) if c == 0 else (ga_s, gb_s)

        def piece(r, carry, c=c, g_prev=g_prev, g_cur=g_cur):
            p = c * per_chunk + r
            rows = pl.ds(pl.multiple_of(r * PEER_PIECE, PEER_PIECE), PEER_PIECE)
            v_rows = vkeep_s[rows, :] if c == 0 else vt_ref[rows, 0:PEER_ECHUNK]
            for th in range(2):
                ts = slice(th * PEER_UNIT_TOKENS, (th + 1) * PEER_UNIT_TOKENS)
                act_unit(p + th, 1 - th)
                acc_s[rows, ts] += jnp.dot(v_rows, g_prev[:, ts], preferred_element_type=F32)
                _gate_unit(p * keys_per_piece, act_s.at[th], need_ref, e1_ref, s2_ref, e2_ref, g_cur,
                           r * PEER_PIECE, th)
            return carry

        lax.fori_loop(0, per_chunk, piece, 0)
    vkeep_s[...] = vt_ref[:, PEER_ECHUNK:2 * PEER_ECHUNK]

    @pl.when(k == pl.num_programs(1) - 1)
    def _():
        acc = acc_s[...] + jnp.dot(vkeep_s[...], gb_s[...], preferred_element_type=F32)
        x2 = x1_ref[...] + acc.T
        y_ref[...] = x2 * lax.rsqrt(jnp.mean(x2 * x2, axis=-1, keepdims=True) + NORM_EPS) * fg_ref[...]


def _peer(xnt, u_bf, vt_bf, need, e1, s2, e2, x1, final_g):
    t = x1.shape[0]
    tt = min(PEER_TOKENS, t)
    assert tt == 2 * PEER_UNIT_TOKENS and need.shape[-1] == PEER_UNIT_TOKENS
    step = 2 * PEER_ECHUNK
    step_keys = step // PEER_NKEYS
    halves = tt // PEER_UNIT_TOKENS
    key_rows = pl.BlockSpec((halves, PEER_HEADS, step_keys, PEER_UNIT_TOKENS), lambda i, e: (i, 0, e, 0))
    all_keys = pl.BlockSpec((halves, PEER_HEADS, PEER_NKEYS, PEER_UNIT_TOKENS), lambda i, e: (i, 0, 0, 0))
    return pl.pallas_call(
        _peer_kernel,
        grid=(t // tt, PEER_EXPERTS // step),
        in_specs=[pl.BlockSpec((halves, D_MODEL, PEER_UNIT_TOKENS), lambda i, e: (i, 0, 0)),
                  pl.BlockSpec((step, D_MODEL), lambda i, e: (e, 0)),
                  pl.BlockSpec((D_MODEL, step), lambda i, e: (0, e)),
                  key_rows, key_rows, all_keys, all_keys,
                  pl.BlockSpec((tt, D_MODEL), lambda i, e: (i, 0)),
                  pl.BlockSpec((1, D_MODEL), lambda i, e: (0, 0))],
        out_specs=pl.BlockSpec((tt, D_MODEL), lambda i, e: (i, 0)),
        out_shape=jax.ShapeDtypeStruct((t, D_MODEL), F32),
        scratch_shapes=[pltpu.VMEM((PEER_ECHUNK, tt), BF16),
                        pltpu.VMEM((PEER_ECHUNK, tt), BF16),
                        pltpu.VMEM((D_MODEL, PEER_ECHUNK), BF16),
                        pltpu.VMEM((D_MODEL, tt), F32),
                        pltpu.VMEM((2, PEER_PIECE, PEER_UNIT_TOKENS), F32)],
        compiler_params=pltpu.CompilerParams(dimension_semantics=("arbitrary",) * 2, vmem_limit_bytes=VMEM_LIMIT),
        name="peer_experts",
    )(xnt, u_bf, vt_bf, need, e1, s2, e2, x1, final_g)


def _alibi_slopes(n):
    start = 2.0 ** (-8.0 / n)
    return jnp.asarray(np.array([start ** (h + 1) for h in range(n)], dtype=np.float32))


def kernel(x, norm1_g, w_in, conv_w, conv_b, b_igate, b_fgate, lam_q1, lam_k1, lam_q2, lam_k2, da_subln_g, ml_norm_g, w_out, norm2_g, w_pq, sub_keys, u_emb, v_emb, final_g):
    bsz, seq, d = x.shape
    t = bsz * seq
    x2 = x.reshape(t, d)
    l = 0
    w_main = w_in[l][:, :N_MAIN].astype(BF16)
    gate_pad = jnp.zeros((d, LANES - ML_HEADS), F32)
    w_gate = jnp.concatenate([w_in[l][:, N_MAIN:N_MAIN + ML_HEADS], gate_pad,
                              w_in[l][:, N_MAIN + ML_HEADS:], gate_pad], axis=1).astype(BF16)
    bias_pad = jnp.zeros((LANES - ML_HEADS,), F32)
    b_gate = jnp.concatenate([b_igate[l], bias_pad, b_fgate[l], bias_pad])[None, :]
    lamv = jnp.stack([lam_q1[l], lam_k1[l], lam_q2[l], lam_k2[l]])

    q0, q1, da_k, da_v, ml_q, ml_k, ml_v, ml_o, gates = _inproj(x2, norm1_g[l][None, :], w_main, w_gate, b_gate)
    seq3 = lambda z: z.reshape(bsz, seq, z.shape[-1])
    a = _diff_attention(seq3(q0), seq3(q1), seq3(da_k), seq3(da_v), _alibi_slopes(DA_HEADS), lamv,
                        da_subln_g[l][None, :])
    hm = _mlstm(seq3(ml_q), seq3(ml_k), seq3(ml_v), seq3(ml_o), seq3(gates),
                conv_w[l], conv_b[l][None, :], ml_norm_g[l][None, :])
    x1, xnt, need, e1, s2, e2 = _mid(a.reshape(t, DA_WIDTH), hm.reshape(t, ML_WIDTH), x2, w_out[l].astype(BF16),
                                     norm2_g[l][None, :], w_pq[l].astype(BF16),
                                     sub_keys[l].reshape(2 * PEER_HEADS, PEER_NKEYS, PEER_HALF).astype(BF16))
    y = _peer(xnt, u_emb[l].astype(BF16), v_emb[l].T.astype(BF16), need, e1, s2, e2, x1, final_g[None, :])
    return y.reshape(bsz, seq, d)
```

```python
import functools

import numpy as np
import jax
import jax.numpy as jnp
from jax import lax
from jax.experimental import pallas as pl
from jax.experimental.pallas import tpu as pltpu

F32 = jnp.float32
BF16 = jnp.bfloat16

D_MODEL = 1024
DA_HEADS = 4
DA_QK_DIM = 64
DA_V_DIM = 128
DA_WIDTH = 512
ML_HEADS = 4
ML_HEAD_DIM = 128
ML_WIDTH = 512
ML_CONV = 4
N_MAIN = 7 * 512
PEER_HEADS = 8
PEER_NKEYS = 128
PEER_EXPERTS = PEER_NKEYS * PEER_NKEYS
PEER_HALF = 128
PEER_TOPK = 16
NORM_EPS = 1e-6
SUBLN_EPS = 1e-5
NEG_INF = -1e30
LAM_INIT = 0.2
LANES = 128
SUBLANES = 8

IN_TOKENS = 512
DA_TQ = 256
DA_TK = 512
DA_HEADS_PER_STEP = 2
ML_CHUNK = 256
MID_TOKENS = 512
PEER_TOKENS = 512
PEER_ECHUNK = 1024
PEER_PIECE = 1024
PEER_UNIT_TOKENS = 256
CAND_COUNTS = tuple(PEER_TOPK // (i + 1) for i in range(PEER_TOPK))
CAND_OFFSETS = tuple(sum(CAND_COUNTS[:i]) for i in range(PEER_TOPK))
N_CAND = sum(CAND_COUNTS)
N_CAND_PAD = -(-N_CAND // SUBLANES) * SUBLANES
VMEM_LIMIT = 56 * 1024 * 1024


def _nt_dot(a, b):
    return lax.dot_general(a, b, (((1,), (1,)), ((), ())), preferred_element_type=F32)


def _inproj_kernel(x_ref, g_ref, w_ref, wg_ref, bg_ref,
                   q0_ref, q1_ref, dak_ref, dav_ref, mlq_ref, mlk_ref, mlv_ref, mlo_ref, gate_ref):
    x = x_ref[...]
    h = x * lax.rsqrt(jnp.mean(x * x, axis=-1, keepdims=True) + NORM_EPS) * g_ref[...]
    hb = h.astype(BF16)

    def mm(group):
        return jnp.dot(hb, w_ref[:, group * 512:(group + 1) * 512], preferred_element_type=F32)

    q = mm(0) * (DA_QK_DIM ** -0.5)
    lane = lax.broadcasted_iota(jnp.int32, q.shape, 1)
    first = (lane % (2 * DA_QK_DIM)) < DA_QK_DIM
    q0_ref[...] = jnp.where(first, q, 0.0).astype(BF16)
    q1_ref[...] = jnp.where(first, 0.0, q).astype(BF16)
    dak_ref[...] = mm(1).astype(BF16)
    dav_ref[...] = mm(2).astype(BF16)
    mlq_ref[...] = mm(3)
    mlk_ref[...] = mm(4)
    mlv_ref[...] = mm(5).astype(BF16)
    mlo_ref[...] = mm(6)
    gate_ref[...] = jnp.dot(hb, wg_ref[...], preferred_element_type=F32) + bg_ref[...]


def _inproj(x2, norm1_g, w_main, w_gate, b_gate):
    t = x2.shape[0]
    tt = min(IN_TOKENS, t)
    row = lambda i: (i, 0)
    fixed = lambda i: (0, 0)
    wide = lambda dt: jax.ShapeDtypeStruct((t, 512), dt)
    return pl.pallas_call(
        _inproj_kernel,
        grid=(t // tt,),
        in_specs=[pl.BlockSpec((tt, D_MODEL), row),
                  pl.BlockSpec((1, D_MODEL), fixed),
                  pl.BlockSpec((D_MODEL, N_MAIN), fixed),
                  pl.BlockSpec((D_MODEL, 2 * LANES), fixed),
                  pl.BlockSpec((1, 2 * LANES), fixed)],
        out_specs=[pl.BlockSpec((tt, 512), row)] * 8 + [pl.BlockSpec((tt, 2 * LANES), row)],
        out_shape=[wide(BF16), wide(BF16), wide(BF16), wide(BF16), wide(F32), wide(F32), wide(BF16), wide(F32),
                   jax.ShapeDtypeStruct((t, 2 * LANES), F32)],
        compiler_params=pltpu.CompilerParams(dimension_semantics=("arbitrary",), vmem_limit_bytes=VMEM_LIMIT),
        name="inproj",
    )(x2, norm1_g, w_main, w_gate, b_gate)


def _da_kernel(slope_ref, lamv_ref, g_ref, q0_ref, q1_ref, k_ref, v_ref, o_ref, m_s, acc_s, s_s):
    hp = pl.program_id(1)
    i = pl.program_id(2)
    tq = q0_ref.shape[0]
    m_s[...] = jnp.full(m_s.shape, NEG_INF, F32)
    acc_s[...] = jnp.zeros(acc_s.shape, F32)
    one_col = jnp.where(lax.broadcasted_iota(jnp.int32, (DA_TK, LANES), 1) == 0, 1.0, 0.0).astype(BF16)

    def scores(j, slot):
        kstart = pl.multiple_of(j * DA_TK, DA_TK)
        krel = (kstart - i * tq + lax.broadcasted_iota(jnp.int32, (1, DA_TK), 1)).astype(F32)
        for hh in range(DA_HEADS_PER_STEP):
            hs = slice(hh * LANES, (hh + 1) * LANES)
            q2 = jnp.concatenate([q0_ref[:, hs], q1_ref[:, hs]], axis=0)
            s_s[slot, hh] = (_nt_dot(q2, k_ref[pl.ds(kstart, DA_TK), hs])
                             + slope_ref[hp * DA_HEADS_PER_STEP + hh] * krel)

    def softmax_pv(j, slot, masked):
        kstart = pl.multiple_of(j * DA_TK, DA_TK)
        for hh in range(DA_HEADS_PER_STEP):
            hs = slice(hh * LANES, (hh + 1) * LANES)
            v_ext = jnp.concatenate([v_ref[pl.ds(kstart, DA_TK), hs], one_col], axis=1)
            s = s_s[slot, hh]
            if masked:
                row = lax.broadcasted_iota(jnp.int32, s.shape, 0)
                col = lax.broadcasted_iota(jnp.int32, s.shape, 1)
                rq = jnp.where(row >= tq, row - tq, row)
                s = jnp.where(col + (kstart - i * tq) <= rq, s, NEG_INF)
            m_prev = m_s[hh]
            m_new = jnp.maximum(m_prev, jnp.max(s, axis=-1, keepdims=True))
            alpha = jnp.exp(m_prev - m_new)
            p = jnp.exp(s - m_new).astype(BF16)
            acc_s[hh] = alpha * acc_s[hh] + jnp.dot(p, v_ext, preferred_element_type=F32)
            m_s[hh] = m_new

    last = (i * tq) // DA_TK

    def block(j, slot):
        @pl.when(j < last)
        def _():
            scores(j + 1, 1 - slot)
            softmax_pv(j, slot, False)

        @pl.when(j == last)
        def _():
            softmax_pv(j, slot, True)

    def pair(jj, carry):
        block(2 * jj, 0)
        block(2 * jj + 1, 1)
        return carry

    scores(0, 0)
    lax.fori_loop(0, last // 2 + 1, pair, 0)

    lamv = lamv_ref[...]
    lam = (jnp.exp(jnp.sum(lamv[0:1] * lamv[1:2], axis=-1, keepdims=True))
           - jnp.exp(jnp.sum(lamv[2:3] * lamv[3:4], axis=-1, keepdims=True)) + LAM_INIT)
    for hh in range(DA_HEADS_PER_STEP):
        acc = acc_s[hh]
        o = acc[:, 0:DA_V_DIM] / acc[:, DA_V_DIM:DA_V_DIM + 1]
        a = o[:tq] - lam * o[tq:]
        y = a * lax.rsqrt(jnp.mean(a * a, axis=-1, keepdims=True) + SUBLN_EPS) * g_ref[...]
        o_ref[:, hh * LANES:(hh + 1) * LANES] = (y * (1.0 - LAM_INIT)).astype(BF16)


def _diff_attention(q0, q1, k, v, slopes, lamv, subln_g):
    b, s, _ = q0.shape
    assert DA_TK % DA_TQ == 0 and s % DA_TK == 0
    hw = DA_HEADS_PER_STEP * LANES
    qspec = pl.BlockSpec((None, DA_TQ, hw), lambda bi, hi, qi: (bi, qi, hi))
    kvspec = pl.BlockSpec((None, s, hw), lambda bi, hi, qi: (bi, 0, hi))
    return pl.pallas_call(
        _da_kernel,
        grid=(b, DA_HEADS // DA_HEADS_PER_STEP, s // DA_TQ),
        in_specs=[pl.BlockSpec(memory_space=pltpu.SMEM),
                  pl.BlockSpec((4, DA_QK_DIM), lambda bi, hi, qi: (0, 0)),
                  pl.BlockSpec((1, DA_V_DIM), lambda bi, hi, qi: (0, 0)),
                  qspec, qspec, kvspec, kvspec],
        out_specs=qspec,
        out_shape=jax.ShapeDtypeStruct((b, s, DA_WIDTH), BF16),
        scratch_shapes=[pltpu.VMEM((DA_HEADS_PER_STEP, 2 * DA_TQ, 1), F32),
                        pltpu.VMEM((DA_HEADS_PER_STEP, 2 * DA_TQ, 2 * DA_V_DIM), F32),
                        pltpu.VMEM((2, DA_HEADS_PER_STEP, 2 * DA_TQ, DA_TK), F32)],
        compiler_params=pltpu.CompilerParams(dimension_semantics=("arbitrary",) * 3, vmem_limit_bytes=VMEM_LIMIT),
        name="diff_attention",
    )(slopes, lamv, subln_g, q0, q1, k, v)


def _mlstm_kernel(tril_ref, cw_ref, cb_ref, ng_ref, q_ref, k_ref, v_ref, og_ref, gate_ref, out_ref,
                  xbuf, cn_s, m_s):
    c = pl.program_id(1)
    L = q_ref.shape[0]

    @pl.when(c == 0)
    def _():
        xbuf[0:SUBLANES, :] = jnp.zeros((SUBLANES, 2 * ML_WIDTH), F32)
        cn_s[...] = jnp.zeros(cn_s.shape, F32)
        m_s[...] = jnp.full(m_s.shape, NEG_INF, F32)

    @pl.when(c > 0)
    def _():
        xbuf[0:SUBLANES, :] = xbuf[L:L + SUBLANES, :]

    xbuf[SUBLANES:SUBLANES + L, 0:ML_WIDTH] = q_ref[...]
    xbuf[SUBLANES:SUBLANES + L, ML_WIDTH:2 * ML_WIDTH] = k_ref[...]
    y = cb_ref[...] + cw_ref[0:1, :] * xbuf[SUBLANES - 3:SUBLANES - 3 + L, :]
    for j in range(1, ML_CONV):
        y = y + cw_ref[j:j + 1, :] * xbuf[SUBLANES - 3 + j:SUBLANES - 3 + j + L, :]
    qk = y * jax.nn.sigmoid(y)

    gates = gate_ref[...]
    i_log = gates[:, 0:LANES]
    f_log = jax.nn.log_sigmoid(gates[:, LANES:2 * LANES])
    b_all = jnp.dot(tril_ref[...], f_log, preferred_element_type=F32,
                    precision=lax.Precision.HIGHEST)
    r_all = i_log - b_all
    r_all_t = r_all.T
    row = lax.broadcasted_iota(jnp.int32, (L, L), 0)
    col = lax.broadcasted_iota(jnp.int32, (L, L), 1)
    tri = col <= row
    one_col = jnp.where(lax.broadcasted_iota(jnp.int32, (L, LANES), 1) == 0, 1.0, 0.0).astype(BF16)

    for h in range(ML_HEADS):
        hs = slice(h * ML_HEAD_DIM, (h + 1) * ML_HEAD_DIM)
        b_col = b_all[:, h:h + 1]
        r_col = r_all[:, h:h + 1]
        r_row = r_all_t[h:h + 1, :]
        m_prev = m_s[h:h + 1, 0:1]
        d = jnp.where(tri, b_col + r_row, NEG_INF)
        inter = b_col + m_prev
        mt = jnp.maximum(inter, jnp.max(d, axis=-1, keepdims=True))
        w_intra = jnp.exp(d - mt)
        w_inter = jnp.exp(inter - mt)
        qb = qk[:, hs].astype(BF16)
        kf = qk[:, ML_WIDTH + h * ML_HEAD_DIM:ML_WIDTH + (h + 1) * ML_HEAD_DIM] * (ML_HEAD_DIM ** -0.5)
        s = _nt_dot(qb, kf.astype(BF16)) * w_intra
        v_ext = jnp.concatenate([v_ref[:, hs], one_col], axis=1)
        cn = cn_s[h]
        num_ext = (w_inter * jnp.dot(qb, cn.astype(BF16), preferred_element_type=F32)
                   + jnp.dot(s.astype(BF16), v_ext, preferred_element_type=F32))
        num = num_ext[:, 0:ML_HEAD_DIM]
        den = num_ext[:, ML_HEAD_DIM:ML_HEAD_DIM + 1]
        hh = num / jnp.maximum(jnp.abs(den), jnp.exp(-mt))
        m_new = mt[L - 1:L, :]
        b_last = b_col[L - 1:L, :]
        ws = jnp.exp(b_last + r_col - m_new)
        decay = jnp.exp(b_last + m_prev - m_new)
        kw_t = (kf * ws).T.astype(BF16)
        cn_s[h] = decay * cn + jnp.dot(kw_t, v_ext, preferred_element_type=F32)
        m_s[h:h + 1, :] = jnp.broadcast_to(m_new, (1, LANES))
        hn = hh * lax.rsqrt(jnp.mean(hh * hh, axis=-1, keepdims=True) + NORM_EPS) * ng_ref[:, hs]
        out_ref[:, hs] = (hn * jax.nn.sigmoid(og_ref[:, hs])).astype(BF16)


def _mlstm(ml_q, ml_k, ml_v, ml_o, gates, conv_w, conv_b, norm_g):
    b, s, _ = ml_q.shape
    L = min(ML_CHUNK, s)
    tril = jnp.tril(jnp.ones((L, L), F32))
    fixed = lambda bi, ci: (0, 0)
    seq = lambda w: pl.BlockSpec((None, L, w), lambda bi, ci: (bi, ci, 0))
    return pl.pallas_call(
        _mlstm_kernel,
        grid=(b, s // L),
        in_specs=[pl.BlockSpec((L, L), fixed),
                  pl.BlockSpec((ML_CONV, 2 * ML_WIDTH), fixed),
                  pl.BlockSpec((1, 2 * ML_WIDTH), fixed),
                  pl.BlockSpec((1, ML_WIDTH), fixed),
                  seq(ML_WIDTH), seq(ML_WIDTH), seq(ML_WIDTH), seq(ML_WIDTH), seq(2 * LANES)],
        out_specs=seq(ML_WIDTH),
        out_shape=jax.ShapeDtypeStruct((b, s, ML_WIDTH), BF16),
        scratch_shapes=[pltpu.VMEM((L + 2 * SUBLANES, 2 * ML_WIDTH), F32),
                        pltpu.VMEM((ML_HEADS, ML_HEAD_DIM, 2 * ML_HEAD_DIM), F32),
                        pltpu.VMEM((SUBLANES, LANES), F32)],
        compiler_params=pltpu.CompilerParams(dimension_semantics=("arbitrary",) * 2, vmem_limit_bytes=VMEM_LIMIT),
        name="mlstm",
    )(tril, conv_w, conv_b, norm_g, ml_q, ml_k, ml_v, ml_o, gates)


def _extract_top_exact(s, n, out_ref, lead):
    rows = s.shape[0]
    iota = lax.broadcasted_iota(jnp.int32, s.shape, 0)
    for k in range(n):
        mx = jnp.max(s, axis=0, keepdims=True)
        out_ref[lead, k:k + 1, :] = mx
        first = jnp.min(jnp.where(s == mx, iota, rows), axis=0, keepdims=True)
        s = jnp.where(iota == first, -jnp.inf, s)
    return s


def _extract_top_fast(s, n, out_ref, lead, n_pad=0):
    for k in range(n):
        mx = jnp.max(s, axis=0, keepdims=True)
        out_ref[lead, k:k + 1, :] = mx
        s = jnp.where(s == mx, -jnp.inf, s)
    removed = jnp.sum(jnp.where(s == -jnp.inf, 1.0, 0.0), axis=0, keepdims=True)
    return s, removed - float(n + n_pad)


def _mid_kernel(a_ref, hm_ref, x_ref, wout_ref, g2_ref, wpq_ref, sk_ref,
                x1_ref, xnt_ref, need_ref, e1_ref, s2_ref, e2_ref, q_s, vals_s, cand_s):
    x1 = (x_ref[...]
          + jnp.dot(a_ref[...], wout_ref[0:DA_WIDTH, :], preferred_element_type=F32)
          + jnp.dot(hm_ref[...], wout_ref[DA_WIDTH:DA_WIDTH + ML_WIDTH, :], preferred_element_type=F32))
    x1_ref[...] = x1
    xn = x1 * lax.rsqrt(jnp.mean(x1 * x1, axis=-1, keepdims=True) + NORM_EPS) * g2_ref[...]
    xnt = xn.T.astype(BF16)
    tiles = [slice(i * PEER_UNIT_TOKENS, (i + 1) * PEER_UNIT_TOKENS) for i in range(xnt.shape[1] // PEER_UNIT_TOKENS)]
    for i, ts in enumerate(tiles):
        xnt_ref[i] = xnt[:, ts]
    q = jnp.dot(xn.astype(BF16), wpq_ref[...], preferred_element_type=F32).astype(BF16)
    for hc in range(2 * PEER_HEADS):
        q_s[hc] = q[:, hc * PEER_HALF:(hc + 1) * PEER_HALF]
    tt = x1.shape[0]
    cand_s[N_CAND:N_CAND_PAD, :] = jnp.full((N_CAND_PAD - N_CAND, tt), -jnp.inf, F32)

    def head(h, carry):
        s1 = _nt_dot(sk_ref[2 * h], q_s[2 * h])
        s2 = _nt_dot(sk_ref[2 * h + 1], q_s[2 * h + 1])

        def select(extract):
            r1 = extract(s1, 0)
            r2 = extract(s2, 1)
            for i, cnt in enumerate(CAND_COUNTS):
                off = CAND_OFFSETS[i]
                cand_s[off:off + cnt, :] = vals_s[0, i:i + 1, :] + vals_s[1, 0:cnt, :]
            r3 = extract(cand_s[...], 2, N_CAND_PAD - N_CAND)
            cand_s[...] = r3[0]
            return jnp.maximum(jnp.maximum(r1[1], r2[1]), r3[1])

        def fast(s, lead, n_pad=0):
            return _extract_top_fast(s, PEER_TOPK, vals_s, lead, n_pad)

        def exact(s, lead, n_pad=0):
            return _extract_top_exact(s, PEER_TOPK, vals_s, lead), jnp.zeros((1, tt), F32)

        excess = select(fast)

        @pl.when(jnp.max(excess) > 0.0)
        def _():
            select(exact)

        top = vals_s[2, 0:PEER_TOPK, :]
        t0 = top[0:1, :]
        lse = t0 + jnp.log(jnp.sum(jnp.exp(top - t0), axis=0, keepdims=True))
        need = jnp.full(s1.shape, jnp.inf, F32)
        for i in reversed(range(PEER_TOPK)):
            cnt, off = CAND_COUNTS[i], CAND_OFFSETS[i]
            cut = jnp.min(jnp.where(cand_s[off:off + cnt, :] == -jnp.inf, vals_s[1, 0:cnt, :], jnp.inf),
                          axis=0, keepdims=True)
            need = jnp.where(s1 >= vals_s[0, i:i + 1, :], cut, need)
        c1 = vals_s[0, 0:1, :]
        e1 = jnp.exp(s1 - c1)
        e2 = 0.5 * jnp.exp(s2 - (lse - c1))
        for i, ts in enumerate(tiles):
            need_ref[i, h] = need[:, ts]
            e1_ref[i, h] = e1[:, ts]
            s2_ref[i, h] = s2[:, ts]
            e2_ref[i, h] = e2[:, ts]
        return carry

    lax.fori_loop(0, PEER_HEADS, head, 0)


def _mid(a, hm, x2, w_out, norm2_g, w_pq, sub_keys):
    t = x2.shape[0]
    tt = min(MID_TOKENS, t)
    tile = PEER_UNIT_TOKENS
    n_tile = tt // tile
    row = lambda i: (i, 0)
    fixed = lambda i: (0, 0)
    return pl.pallas_call(
        _mid_kernel,
        grid=(t // tt,),
        in_specs=[pl.BlockSpec((tt, DA_WIDTH), row),
                  pl.BlockSpec((tt, ML_WIDTH), row),
                  pl.BlockSpec((tt, D_MODEL), row),
                  pl.BlockSpec((D_MODEL, D_MODEL), fixed),
                  pl.BlockSpec((1, D_MODEL), fixed),
                  pl.BlockSpec((D_MODEL, 2 * PEER_HEADS * PEER_HALF), fixed),
                  pl.BlockSpec((2 * PEER_HEADS, PEER_NKEYS, PEER_HALF), lambda i: (0, 0, 0))],
        out_specs=[pl.BlockSpec((tt, D_MODEL), row),
                   pl.BlockSpec((n_tile, D_MODEL, tile), lambda i: (i, 0, 0))]
                  + [pl.BlockSpec((n_tile, PEER_HEADS, PEER_NKEYS, tile), lambda i: (i, 0, 0, 0))] * 4,
        out_shape=[jax.ShapeDtypeStruct((t, D_MODEL), F32),
                   jax.ShapeDtypeStruct((t // tile, D_MODEL, tile), BF16)]
                  + [jax.ShapeDtypeStruct((t // tile, PEER_HEADS, PEER_NKEYS, tile), F32)] * 4,
        scratch_shapes=[pltpu.VMEM((2 * PEER_HEADS, tt, PEER_HALF), BF16),
                        pltpu.VMEM((3, 2 * SUBLANES, tt), F32),
                        pltpu.VMEM((N_CAND_PAD, tt), F32)],
        compiler_params=pltpu.CompilerParams(dimension_semantics=("arbitrary",), vmem_limit_bytes=VMEM_LIMIT),
        name="mid",
    )(a, hm, x2, w_out, norm2_g, w_pq, sub_keys)


def _gate_unit(key0, act_ref, need_ref, e1_ref, s2_ref, e2_ref, g_ref, row0, th):
    for q in range(PEER_PIECE // PEER_NKEYS):
        need_rows = [need_ref[th, h, pl.ds(key0 + q, 1), :] for h in range(PEER_HEADS)]
        e1_rows = [e1_ref[th, h, pl.ds(key0 + q, 1), :] for h in range(PEER_HEADS)]
        for lt in range(PEER_UNIT_TOKENS // LANES):
            la = slice(lt * LANES, (lt + 1) * LANES)
            ls = slice(th * PEER_UNIT_TOKENS + lt * LANES, th * PEER_UNIT_TOKENS + (lt + 1) * LANES)
            w = None
            for h in range(PEER_HEADS):
                wh = jnp.where(s2_ref[th, h, :, la] >= need_rows[h][:, la], e2_ref[th, h, :, la], 0.0) * e1_rows[h][:, la]
                w = wh if w is None else w + wh
            a = act_ref[q * PEER_NKEYS:(q + 1) * PEER_NKEYS, la]
            g_ref[pl.ds(pl.multiple_of(row0 + q * PEER_NKEYS, PEER_NKEYS), PEER_NKEYS), ls] = (
                a * (1.0 + lax.erf(a * np.float32(np.sqrt(0.5)))) * w).astype(BF16)


def _peer_kernel(xnt_ref, u_ref, vt_ref, need_ref, e1_ref, s2_ref, e2_ref, x1_ref, fg_ref, y_ref,
                 ga_s, gb_s, vkeep_s, acc_s, act_s):
    k = pl.program_id(1)

    @pl.when(k == 0)
    def _():
        acc_s[...] = jnp.zeros(acc_s.shape, F32)
        gb_s[...] = jnp.zeros(gb_s.shape, BF16)
        vkeep_s[...] = jnp.zeros(vkeep_s.shape, BF16)

    per_chunk = PEER_ECHUNK // PEER_PIECE
    n_piece = 2 * per_chunk
    keys_per_piece = PEER_PIECE // PEER_NKEYS

    def act_unit(p, th):
        off = pl.multiple_of(jnp.minimum(p, n_piece - 1) * PEER_PIECE, PEER_PIECE)
        act_s[th] = jnp.dot(u_ref[pl.ds(off, PEER_PIECE), :], xnt_ref[th], preferred_element_type=F32)

    act_unit(0, 0)
    for c in range(2):
        g_prev, g_cur = (gb_s, ga_s) if c == 0 else (ga_s, gb_s)

        def piece(r, carry, c=c, g_prev=g_prev, g_cur=g_cur):
            p = c * per_chunk + r
            rows = pl.ds(pl.multiple_of(r * PEER_PIECE, PEER_PIECE), PEER_PIECE)
            v_rows = vkeep_s[rows, :] if c == 0 else vt_ref[rows, 0:PEER_ECHUNK]
            for th in range(2):
                ts = slice(th * PEER_UNIT_TOKENS, (th + 1) * PEER_UNIT_TOKENS)
                act_unit(p + th, 1 - th)
                acc_s[rows, ts] += jnp.dot(v_rows, g_prev[:, ts], preferred_element_type=F32)
                _gate_unit(p * keys_per_piece, act_s.at[th], need_ref, e1_ref, s2_ref, e2_ref, g_cur,
                           r * PEER_PIECE, th)
            return carry

        lax.fori_loop(0, per_chunk, piece, 0)
    vkeep_s[...] = vt_ref[:, PEER_ECHUNK:2 * PEER_ECHUNK]

    @pl.when(k == pl.num_programs(1) - 1)
    def _():
        acc = acc_s[...] + jnp.dot(vkeep_s[...], gb_s[...], preferred_element_type=F32)
        x2 = x1_ref[...] + acc.T
        y_ref[...] = x2 * lax.rsqrt(jnp.mean(x2 * x2, axis=-1, keepdims=True) + NORM_EPS) * fg_ref[...]


def _peer(xnt, u_bf, vt_bf, need, e1, s2, e2, x1, final_g):
    t = x1.shape[0]
    tt = min(PEER_TOKENS, t)
    assert tt == 2 * PEER_UNIT_TOKENS and need.shape[-1] == PEER_UNIT_TOKENS
    step = 2 * PEER_ECHUNK
    step_keys = step // PEER_NKEYS
    halves = tt // PEER_UNIT_TOKENS
    key_rows = pl.BlockSpec((halves, PEER_HEADS, step_keys, PEER_UNIT_TOKENS), lambda i, e: (i, 0, e, 0))
    all_keys = pl.BlockSpec((halves, PEER_HEADS, PEER_NKEYS, PEER_UNIT_TOKENS), lambda i, e: (i, 0, 0, 0))
    return pl.pallas_call(
        _peer_kernel,
        grid=(t // tt, PEER_EXPERTS // step),
        in_specs=[pl.BlockSpec((halves, D_MODEL, PEER_UNIT_TOKENS), lambda i, e: (i, 0, 0)),
                  pl.BlockSpec((step, D_MODEL), lambda i, e: (e, 0)),
                  pl.BlockSpec((D_MODEL, step), lambda i, e: (0, e)),
                  key_rows, key_rows, all_keys, all_keys,
                  pl.BlockSpec((tt, D_MODEL), lambda i, e: (i, 0)),
                  pl.BlockSpec((1, D_MODEL), lambda i, e: (0, 0))],
        out_specs=pl.BlockSpec((tt, D_MODEL), lambda i, e: (i, 0)),
        out_shape=jax.ShapeDtypeStruct((t, D_MODEL), F32),
        scratch_shapes=[pltpu.VMEM((PEER_ECHUNK, tt), BF16),
                        pltpu.VMEM((PEER_ECHUNK, tt), BF16),
                        pltpu.VMEM((D_MODEL, PEER_ECHUNK), BF16),
                        pltpu.VMEM((D_MODEL, tt), F32),
                        pltpu.VMEM((2, PEER_PIECE, PEER_UNIT_TOKENS), F32)],
        compiler_params=pltpu.CompilerParams(dimension_semantics=("arbitrary",) * 2, vmem_limit_bytes=VMEM_LIMIT),
        name="peer_experts",
    )(xnt, u_bf, vt_bf, need, e1, s2, e2, x1, final_g)


def _alibi_slopes(n):
    start = 2.0 ** (-8.0 / n)
    return jnp.asarray(np.array([start ** (h + 1) for h in range(n)], dtype=np.float32))


def kernel(x, norm1_g, w_in, conv_w, conv_b, b_igate, b_fgate, lam_q1, lam_k1, lam_q2, lam_k2, da_subln_g, ml_norm_g, w_out, norm2_g, w_pq, sub_keys, u_emb, v_emb, final_g):
    bsz, seq, d = x.shape
    t = bsz * seq
    x2 = x.reshape(t, d)
    l = 0
    w_main = w_in[l][:, :N_MAIN].astype(BF16)
    gate_pad = jnp.zeros((d, LANES - ML_HEADS), F32)
    w_gate = jnp.concatenate([w_in[l][:, N_MAIN:N_MAIN + ML_HEADS], gate_pad,
                              w_in[l][:, N_MAIN + ML_HEADS:], gate_pad], axis=1).astype(BF16)
    bias_pad = jnp.zeros((LANES - ML_HEADS,), F32)
    b_gate = jnp.concatenate([b_igate[l], bias_pad, b_fgate[l], bias_pad])[None, :]
    lamv = jnp.stack([lam_q1[l], lam_k1[l], lam_q2[l], lam_k2[l]])

    q0, q1, da_k, da_v, ml_q, ml_k, ml_v, ml_o, gates = _inproj(x2, norm1_g[l][None, :], w_main, w_gate, b_gate)
    seq3 = lambda z: z.reshape(bsz, seq, z.shape[-1])
    a = _diff_attention(seq3(q0), seq3(q1), seq3(da_k), seq3(da_v), _alibi_slopes(DA_HEADS), lamv,
                        da_subln_g[l][None, :])
    hm = _mlstm(seq3(ml_q), seq3(ml_k), seq3(ml_v), seq3(ml_o), seq3(gates),
                conv_w[l], conv_b[l][None, :], ml_norm_g[l][None, :])
    x1, xnt, need, e1, s2, e2 = _mid(a.reshape(t, DA_WIDTH), hm.reshape(t, ML_WIDTH), x2, w_out[l].astype(BF16),
                                     norm2_g[l][None, :], w_pq[l].astype(BF16),
                                     sub_keys[l].reshape(2 * PEER_HEADS, PEER_NKEYS, PEER_HALF).astype(BF16))
    y = _peer(xnt, u_emb[l].astype(BF16), v_emb[l].T.astype(BF16), need, e1, s2, e2, x1, final_g[None, :])
    return y.reshape(bsz, seq, d)
```

```python
import functools

import numpy as np
import jax
import jax.numpy as jnp
from jax import lax
from jax.experimental import pallas as pl
from jax.experimental.pallas import tpu as pltpu

F32 = jnp.float32
BF16 = jnp.bfloat16

D_MODEL = 1024
DA_HEADS = 4
DA_QK_DIM = 64
DA_V_DIM = 128
DA_WIDTH = 512
ML_HEADS = 4
ML_HEAD_DIM = 128
ML_WIDTH = 512
ML_CONV = 4
N_MAIN = 7 * 512
PEER_HEADS = 8
PEER_NKEYS = 128
PEER_EXPERTS = PEER_NKEYS * PEER_NKEYS
PEER_HALF = 128
PEER_TOPK = 16
NORM_EPS = 1e-6
SUBLN_EPS = 1e-5
NEG_INF = -1e30
LAM_INIT = 0.2
LANES = 128
SUBLANES = 8

IN_TOKENS = 512
DA_TQ = 256
DA_TK = 512
DA_HEADS_PER_STEP = 2
ML_CHUNK = 256
MID_TOKENS = 512
PEER_TOKENS = 512
PEER_ECHUNK = 1024
PEER_PIECE = 512
PEER_UNIT_TOKENS = 256
CAND_COUNTS = tuple(PEER_TOPK // (i + 1) for i in range(PEER_TOPK))
CAND_OFFSETS = tuple(sum(CAND_COUNTS[:i]) for i in range(PEER_TOPK))
N_CAND = sum(CAND_COUNTS)
N_CAND_PAD = -(-N_CAND // SUBLANES) * SUBLANES
VMEM_LIMIT = 56 * 1024 * 1024


def _nt_dot(a, b):
    return lax.dot_general(a, b, (((1,), (1,)), ((), ())), preferred_element_type=F32)


def _inproj_kernel(x_ref, g_ref, w_ref, wg_ref, bg_ref,
                   q0_ref, q1_ref, dak_ref, dav_ref, mlq_ref, mlk_ref, mlv_ref, mlo_ref, gate_ref):
    x = x_ref[...]
    h = x * lax.rsqrt(jnp.mean(x * x, axis=-1, keepdims=True) + NORM_EPS) * g_ref[...]
    hb = h.astype(BF16)

    def mm(group):
        return jnp.dot(hb, w_ref[:, group * 512:(group + 1) * 512], preferred_element_type=F32)

    q = mm(0) * (DA_QK_DIM ** -0.5)
    lane = lax.broadcasted_iota(jnp.int32, q.shape, 1)
    first = (lane % (2 * DA_QK_DIM)) < DA_QK_DIM
    q0_ref[...] = jnp.where(first, q, 0.0).astype(BF16)
    q1_ref[...] = jnp.where(first, 0.0, q).astype(BF16)
    dak_ref[...] = mm(1).astype(BF16)
    dav_ref[...] = mm(2).astype(BF16)
    mlq_ref[...] = mm(3)
    mlk_ref[...] = mm(4)
    mlv_ref[...] = mm(5).astype(BF16)
    mlo_ref[...] = mm(6)
    gate_ref[...] = jnp.dot(hb, wg_ref[...], preferred_element_type=F32) + bg_ref[...]


def _inproj(x2, norm1_g, w_main, w_gate, b_gate):
    t = x2.shape[0]
    tt = min(IN_TOKENS, t)
    row = lambda i: (i, 0)
    fixed = lambda i: (0, 0)
    wide = lambda dt: jax.ShapeDtypeStruct((t, 512), dt)
    return pl.pallas_call(
        _inproj_kernel,
        grid=(t // tt,),
        in_specs=[pl.BlockSpec((tt, D_MODEL), row),
                  pl.BlockSpec((1, D_MODEL), fixed),
                  pl.BlockSpec((D_MODEL, N_MAIN), fixed),
                  pl.BlockSpec((D_MODEL, 2 * LANES), fixed),
                  pl.BlockSpec((1, 2 * LANES), fixed)],
        out_specs=[pl.BlockSpec((tt, 512), row)] * 8 + [pl.BlockSpec((tt, 2 * LANES), row)],
        out_shape=[wide(BF16), wide(BF16), wide(BF16), wide(BF16), wide(F32), wide(F32), wide(BF16), wide(F32),
                   jax.ShapeDtypeStruct((t, 2 * LANES), F32)],
        compiler_params=pltpu.CompilerParams(dimension_semantics=("arbitrary",), vmem_limit_bytes=VMEM_LIMIT),
        name="inproj",
    )(x2, norm1_g, w_main, w_gate, b_gate)


def _da_kernel(slope_ref, lamv_ref, g_ref, q0_ref, q1_ref, k_ref, v_ref, o_ref, m_s, acc_s, s_s):
    hp = pl.program_id(1)
    i = pl.program_id(2)
    tq = q0_ref.shape[0]
    m_s[...] = jnp.full(m_s.shape, NEG_INF, F32)
    acc_s[...] = jnp.zeros(acc_s.shape, F32)
    one_col = jnp.where(lax.broadcasted_iota(jnp.int32, (DA_TK, LANES), 1) == 0, 1.0, 0.0).astype(BF16)

    def scores(j, slot):
        kstart = pl.multiple_of(j * DA_TK, DA_TK)
        krel = (kstart - i * tq + lax.broadcasted_iota(jnp.int32, (1, DA_TK), 1)).astype(F32)
        for hh in range(DA_HEADS_PER_STEP):
            hs = slice(hh * LANES, (hh + 1) * LANES)
            q2 = jnp.concatenate([q0_ref[:, hs], q1_ref[:, hs]], axis=0)
            s_s[slot, hh] = (_nt_dot(q2, k_ref[pl.ds(kstart, DA_TK), hs])
                             + slope_ref[hp * DA_HEADS_PER_STEP + hh] * krel)

    def softmax_pv(j, slot, masked):
        kstart = pl.multiple_of(j * DA_TK, DA_TK)
        for hh in range(DA_HEADS_PER_STEP):
            hs = slice(hh * LANES, (hh + 1) * LANES)
            v_ext = jnp.concatenate([v_ref[pl.ds(kstart, DA_TK), hs], one_col], axis=1)
            s = s_s[slot, hh]
            if masked:
                row = lax.broadcasted_iota(jnp.int32, s.shape, 0)
                col = lax.broadcasted_iota(jnp.int32, s.shape, 1)
                rq = jnp.where(row >= tq, row - tq, row)
                s = jnp.where(col + (kstart - i * tq) <= rq, s, NEG_INF)
            m_prev = m_s[hh]
            m_new = jnp.maximum(m_prev, jnp.max(s, axis=-1, keepdims=True))
            alpha = jnp.exp(m_prev - m_new)
            p = jnp.exp(s - m_new).astype(BF16)
            acc_s[hh] = alpha * acc_s[hh] + jnp.dot(p, v_ext, preferred_element_type=F32)
            m_s[hh] = m_new

    last = (i * tq) // DA_TK

    def block(j, slot):
        @pl.when(j < last)
        def _():
            scores(j + 1, 1 - slot)
            softmax_pv(j, slot, False)

        @pl.when(j == last)
        def _():
            softmax_pv(j, slot, True)

    def pair(jj, carry):
        block(2 * jj, 0)
        block(2 * jj + 1, 1)
        return carry

    scores(0, 0)
    lax.fori_loop(0, last // 2 + 1, pair, 0)

    lamv = lamv_ref[...]
    lam = (jnp.exp(jnp.sum(lamv[0:1] * lamv[1:2], axis=-1, keepdims=True))
           - jnp.exp(jnp.sum(lamv[2:3] * lamv[3:4], axis=-1, keepdims=True)) + LAM_INIT)
    for hh in range(DA_HEADS_PER_STEP):
        acc = acc_s[hh]
        o = acc[:, 0:DA_V_DIM] / acc[:, DA_V_DIM:DA_V_DIM + 1]
        a = o[:tq] - lam * o[tq:]
        y = a * lax.rsqrt(jnp.mean(a * a, axis=-1, keepdims=True) + SUBLN_EPS) * g_ref[...]
        o_ref[:, hh * LANES:(hh + 1) * LANES] = (y * (1.0 - LAM_INIT)).astype(BF16)


def _diff_attention(q0, q1, k, v, slopes, lamv, subln_g):
    b, s, _ = q0.shape
    assert DA_TK % DA_TQ == 0 and s % DA_TK == 0
    hw = DA_HEADS_PER_STEP * LANES
    qspec = pl.BlockSpec((None, DA_TQ, hw), lambda bi, hi, qi: (bi, qi, hi))
    kvspec = pl.BlockSpec((None, s, hw), lambda bi, hi, qi: (bi, 0, hi))
    return pl.pallas_call(
        _da_kernel,
        grid=(b, DA_HEADS // DA_HEADS_PER_STEP, s // DA_TQ),
        in_specs=[pl.BlockSpec(memory_space=pltpu.SMEM),
                  pl.BlockSpec((4, DA_QK_DIM), lambda bi, hi, qi: (0, 0)),
                  pl.BlockSpec((1, DA_V_DIM), lambda bi, hi, qi: (0, 0)),
                  qspec, qspec, kvspec, kvspec],
        out_specs=qspec,
        out_shape=jax.ShapeDtypeStruct((b, s, DA_WIDTH), BF16),
        scratch_shapes=[pltpu.VMEM((DA_HEADS_PER_STEP, 2 * DA_TQ, 1), F32),
                        pltpu.VMEM((DA_HEADS_PER_STEP, 2 * DA_TQ, 2 * DA_V_DIM), F32),
                        pltpu.VMEM((2, DA_HEADS_PER_STEP, 2 * DA_TQ, DA_TK), F32)],
        compiler_params=pltpu.CompilerParams(dimension_semantics=("arbitrary",) * 3, vmem_limit_bytes=VMEM_LIMIT),
        name="diff_attention",
    )(slopes, lamv, subln_g, q0, q1, k, v)


def _mlstm_kernel(tril_ref, cw_ref, cb_ref, ng_ref, q_ref, k_ref, v_ref, og_ref, gate_ref, out_ref,
                  xbuf, cn_s, m_s):
    c = pl.program_id(1)
    L = q_ref.shape[0]

    @pl.when(c == 0)
    def _():
        xbuf[0:SUBLANES, :] = jnp.zeros((SUBLANES, 2 * ML_WIDTH), F32)
        cn_s[...] = jnp.zeros(cn_s.shape, F32)
        m_s[...] = jnp.full(m_s.shape, NEG_INF, F32)

    @pl.when(c > 0)
    def _():
        xbuf[0:SUBLANES, :] = xbuf[L:L + SUBLANES, :]

    xbuf[SUBLANES:SUBLANES + L, 0:ML_WIDTH] = q_ref[...]
    xbuf[SUBLANES:SUBLANES + L, ML_WIDTH:2 * ML_WIDTH] = k_ref[...]
    y = cb_ref[...] + cw_ref[0:1, :] * xbuf[SUBLANES - 3:SUBLANES - 3 + L, :]
    for j in range(1, ML_CONV):
        y = y + cw_ref[j:j + 1, :] * xbuf[SUBLANES - 3 + j:SUBLANES - 3 + j + L, :]
    qk = y * jax.nn.sigmoid(y)

    gates = gate_ref[...]
    i_log = gates[:, 0:LANES]
    f_log = jax.nn.log_sigmoid(gates[:, LANES:2 * LANES])
    b_all = jnp.dot(tril_ref[...], f_log, preferred_element_type=F32,
                    precision=lax.Precision.HIGHEST)
    r_all = i_log - b_all
    r_all_t = r_all.T
    row = lax.broadcasted_iota(jnp.int32, (L, L), 0)
    col = lax.broadcasted_iota(jnp.int32, (L, L), 1)
    tri = col <= row
    one_col = jnp.where(lax.broadcasted_iota(jnp.int32, (L, LANES), 1) == 0, 1.0, 0.0).astype(BF16)

    for h in range(ML_HEADS):
        hs = slice(h * ML_HEAD_DIM, (h + 1) * ML_HEAD_DIM)
        b_col = b_all[:, h:h + 1]
        r_col = r_all[:, h:h + 1]
        r_row = r_all_t[h:h + 1, :]
        m_prev = m_s[h:h + 1, 0:1]
        d = jnp.where(tri, b_col + r_row, NEG_INF)
        inter = b_col + m_prev
        mt = jnp.maximum(inter, jnp.max(d, axis=-1, keepdims=True))
        w_intra = jnp.exp(d - mt)
        w_inter = jnp.exp(inter - mt)
        qb = qk[:, hs].astype(BF16)
        kf = qk[:, ML_WIDTH + h * ML_HEAD_DIM:ML_WIDTH + (h + 1) * ML_HEAD_DIM] * (ML_HEAD_DIM ** -0.5)
        s = _nt_dot(qb, kf.astype(BF16)) * w_intra
        v_ext = jnp.concatenate([v_ref[:, hs], one_col], axis=1)
        cn = cn_s[h]
        num_ext = (w_inter * jnp.dot(qb, cn.astype(BF16), preferred_element_type=F32)
                   + jnp.dot(s.astype(BF16), v_ext, preferred_element_type=F32))
        num = num_ext[:, 0:ML_HEAD_DIM]
        den = num_ext[:, ML_HEAD_DIM:ML_HEAD_DIM + 1]
        hh = num / jnp.maximum(jnp.abs(den), jnp.exp(-mt))
        m_new = mt[L - 1:L, :]
        b_last = b_col[L - 1:L, :]
        ws = jnp.exp(b_last + r_col - m_new)
        decay = jnp.exp(b_last + m_prev - m_new)
        kw_t = (kf * ws).T.astype(BF16)
        cn_s[h] = decay * cn + jnp.dot(kw_t, v_ext, preferred_element_type=F32)
        m_s[h:h + 1, :] = jnp.broadcast_to(m_new, (1, LANES))
        hn = hh * lax.rsqrt(jnp.mean(hh * hh, axis=-1, keepdims=True) + NORM_EPS) * ng_ref[:, hs]
        out_ref[:, hs] = (hn * jax.nn.sigmoid(og_ref[:, hs])).astype(BF16)


def _mlstm(ml_q, ml_k, ml_v, ml_o, gates, conv_w, conv_b, norm_g):
    b, s, _ = ml_q.shape
    L = min(ML_CHUNK, s)
    tril = jnp.tril(jnp.ones((L, L), F32))
    fixed = lambda bi, ci: (0, 0)
    seq = lambda w: pl.BlockSpec((None, L, w), lambda bi, ci: (bi, ci, 0))
    return pl.pallas_call(
        _mlstm_kernel,
        grid=(b, s // L),
        in_specs=[pl.BlockSpec((L, L), fixed),
                  pl.BlockSpec((ML_CONV, 2 * ML_WIDTH), fixed),
                  pl.BlockSpec((1, 2 * ML_WIDTH), fixed),
                  pl.BlockSpec((1, ML_WIDTH), fixed),
                  seq(ML_WIDTH), seq(ML_WIDTH), seq(ML_WIDTH), seq(ML_WIDTH), seq(2 * LANES)],
        out_specs=seq(ML_WIDTH),
        out_shape=jax.ShapeDtypeStruct((b, s, ML_WIDTH), BF16),
        scratch_shapes=[pltpu.VMEM((L + 2 * SUBLANES, 2 * ML_WIDTH), F32),
                        pltpu.VMEM((ML_HEADS, ML_HEAD_DIM, 2 * ML_HEAD_DIM), F32),
                        pltpu.VMEM((SUBLANES, LANES), F32)],
        compiler_params=pltpu.CompilerParams(dimension_semantics=("arbitrary",) * 2, vmem_limit_bytes=VMEM_LIMIT),
        name="mlstm",
    )(tril, conv_w, conv_b, norm_g, ml_q, ml_k, ml_v, ml_o, gates)


def _extract_top_exact(s, n, out_ref, lead):
    rows = s.shape[0]
    iota = lax.broadcasted_iota(jnp.int32, s.shape, 0)
    for k in range(n):
        mx = jnp.max(s, axis=0, keepdims=True)
        out_ref[lead, k:k + 1, :] = mx
        first = jnp.min(jnp.where(s == mx, iota, rows), axis=0, keepdims=True)
        s = jnp.where(iota == first, -jnp.inf, s)
    return s


def _extract_top_fast(s, n, out_ref, lead, n_pad=0):
    for k in range(n):
        mx = jnp.max(s, axis=0, keepdims=True)
        out_ref[lead, k:k + 1, :] = mx
        s = jnp.where(s == mx, -jnp.inf, s)
    removed = jnp.sum(jnp.where(s == -jnp.inf, 1.0, 0.0), axis=0, keepdims=True)
    return s, removed - float(n + n_pad)


def _mid_kernel(a_ref, hm_ref, x_ref, wout_ref, g2_ref, wpq_ref, sk_ref,
                x1_ref, xnt_ref, need_ref, e1_ref, s2_ref, e2_ref, q_s, vals_s, cand_s):
    x1 = (x_ref[...]
          + jnp.dot(a_ref[...], wout_ref[0:DA_WIDTH, :], preferred_element_type=F32)
          + jnp.dot(hm_ref[...], wout_ref[DA_WIDTH:DA_WIDTH + ML_WIDTH, :], preferred_element_type=F32))
    x1_ref[...] = x1
    xn = x1 * lax.rsqrt(jnp.mean(x1 * x1, axis=-1, keepdims=True) + NORM_EPS) * g2_ref[...]
    xnt = xn.T.astype(BF16)
    tiles = [slice(i * PEER_UNIT_TOKENS, (i + 1) * PEER_UNIT_TOKENS) for i in range(xnt.shape[1] // PEER_UNIT_TOKENS)]
    for i, ts in enumerate(tiles):
        xnt_ref[i] = xnt[:, ts]
    q = jnp.dot(xn.astype(BF16), wpq_ref[...], preferred_element_type=F32).astype(BF16)
    for hc in range(2 * PEER_HEADS):
        q_s[hc] = q[:, hc * PEER_HALF:(hc + 1) * PEER_HALF]
    tt = x1.shape[0]
    cand_s[N_CAND:N_CAND_PAD, :] = jnp.full((N_CAND_PAD - N_CAND, tt), -jnp.inf, F32)

    def head(h, carry):
        s1 = _nt_dot(sk_ref[2 * h], q_s[2 * h])
        s2 = _nt_dot(sk_ref[2 * h + 1], q_s[2 * h + 1])

        def select(extract):
            r1 = extract(s1, 0)
            r2 = extract(s2, 1)
            for i, cnt in enumerate(CAND_COUNTS):
                off = CAND_OFFSETS[i]
                cand_s[off:off + cnt, :] = vals_s[0, i:i + 1, :] + vals_s[1, 0:cnt, :]
            r3 = extract(cand_s[...], 2, N_CAND_PAD - N_CAND)
            cand_s[...] = r3[0]
            return jnp.maximum(jnp.maximum(r1[1], r2[1]), r3[1])

        def fast(s, lead, n_pad=0):
            return _extract_top_fast(s, PEER_TOPK, vals_s, lead, n_pad)

        def exact(s, lead, n_pad=0):
            return _extract_top_exact(s, PEER_TOPK, vals_s, lead), jnp.zeros((1, tt), F32)

        excess = select(fast)

        @pl.when(jnp.max(excess) > 0.0)
        def _():
            select(exact)

        top = vals_s[2, 0:PEER_TOPK, :]
        t0 = top[0:1, :]
        lse = t0 + jnp.log(jnp.sum(jnp.exp(top - t0), axis=0, keepdims=True))
        need = jnp.full(s1.shape, jnp.inf, F32)
        for i in reversed(range(PEER_TOPK)):
            cnt, off = CAND_COUNTS[i], CAND_OFFSETS[i]
            cut = jnp.min(jnp.where(cand_s[off:off + cnt, :] == -jnp.inf, vals_s[1, 0:cnt, :], jnp.inf),
                          axis=0, keepdims=True)
            need = jnp.where(s1 >= vals_s[0, i:i + 1, :], cut, need)
        c1 = vals_s[0, 0:1, :]
        e1 = jnp.exp(s1 - c1)
        e2 = 0.5 * jnp.exp(s2 - (lse - c1))
        for i, ts in enumerate(tiles):
            need_ref[i, h] = need[:, ts]
            e1_ref[i, h] = e1[:, ts]
            s2_ref[i, h] = s2[:, ts]
            e2_ref[i, h] = e2[:, ts]
        return carry

    lax.fori_loop(0, PEER_HEADS, head, 0)


def _mid(a, hm, x2, w_out, norm2_g, w_pq, sub_keys):
    t = x2.shape[0]
    tt = min(MID_TOKENS, t)
    tile = PEER_UNIT_TOKENS
    n_tile = tt // tile
    row = lambda i: (i, 0)
    fixed = lambda i: (0, 0)
    return pl.pallas_call(
        _mid_kernel,
        grid=(t // tt,),
        in_specs=[pl.BlockSpec((tt, DA_WIDTH), row),
                  pl.BlockSpec((tt, ML_WIDTH), row),
                  pl.BlockSpec((tt, D_MODEL), row),
                  pl.BlockSpec((D_MODEL, D_MODEL), fixed),
                  pl.BlockSpec((1, D_MODEL), fixed),
                  pl.BlockSpec((D_MODEL, 2 * PEER_HEADS * PEER_HALF), fixed),
                  pl.BlockSpec((2 * PEER_HEADS, PEER_NKEYS, PEER_HALF), lambda i: (0, 0, 0))],
        out_specs=[pl.BlockSpec((tt, D_MODEL), row),
                   pl.BlockSpec((n_tile, D_MODEL, tile), lambda i: (i, 0, 0))]
                  + [pl.BlockSpec((n_tile, PEER_HEADS, PEER_NKEYS, tile), lambda i: (i, 0, 0, 0))] * 4,
        out_shape=[jax.ShapeDtypeStruct((t, D_MODEL), F32),
                   jax.ShapeDtypeStruct((t // tile, D_MODEL, tile), BF16)]
                  + [jax.ShapeDtypeStruct((t // tile, PEER_HEADS, PEER_NKEYS, tile), F32)] * 4,
        scratch_shapes=[pltpu.VMEM((2 * PEER_HEADS, tt, PEER_HALF), BF16),
                        pltpu.VMEM((3, 2 * SUBLANES, tt), F32),
                        pltpu.VMEM((N_CAND_PAD, tt), F32)],
        compiler_params=pltpu.CompilerParams(dimension_semantics=("arbitrary",), vmem_limit_bytes=VMEM_LIMIT),
        name="mid",
    )(a, hm, x2, w_out, norm2_g, w_pq, sub_keys)


def _gate_unit(key0, act_ref, need_ref, e1_ref, s2_ref, e2_ref, g_ref, row0, th):
    for q in range(PEER_PIECE // PEER_NKEYS):
        need_rows = [need_ref[th, h, pl.ds(key0 + q, 1), :] for h in range(PEER_HEADS)]
        e1_rows = [e1_ref[th, h, pl.ds(key0 + q, 1), :] for h in range(PEER_HEADS)]
        for lt in range(PEER_UNIT_TOKENS // LANES):
            la = slice(lt * LANES, (lt + 1) * LANES)
            ls = slice(th * PEER_UNIT_TOKENS + lt * LANES, th * PEER_UNIT_TOKENS + (lt + 1) * LANES)
            w = None
            for h in range(PEER_HEADS):
                wh = jnp.where(s2_ref[th, h, :, la] >= need_rows[h][:, la], e2_ref[th, h, :, la], 0.0) * e1_rows[h][:, la]
                w = wh if w is None else w + wh
            a = act_ref[q * PEER_NKEYS:(q + 1) * PEER_NKEYS, la]
            g_ref[pl.ds(pl.multiple_of(row0 + q * PEER_NKEYS, PEER_NKEYS), PEER_NKEYS), ls] = (
                a * (1.0 + lax.erf(a * np.float32(np.sqrt(0.5)))) * w).astype(BF16)


def _peer_kernel(xnt_ref, u_ref, vt_ref, need_ref, e1_ref, s2_ref, e2_ref, x1_ref, fg_ref, y_ref,
                 ga_s, gb_s, vkeep_s, acc_s, act_s):
    k = pl.program_id(1)

    @pl.when(k == 0)
    def _():
        acc_s[...] = jnp.zeros(acc_s.shape, F32)
        gb_s[...] = jnp.zeros(gb_s.shape, BF16)
        vkeep_s[...] = jnp.zeros(vkeep_s.shape, BF16)

    per_chunk = PEER_ECHUNK // PEER_PIECE
    n_piece = 2 * per_chunk
    keys_per_piece = PEER_PIECE // PEER_NKEYS

    def act_unit(p, th):
        off = pl.multiple_of(jnp.minimum(p, n_piece - 1) * PEER_PIECE, PEER_PIECE)
        act_s[th] = jnp.dot(u_ref[pl.ds(off, PEER_PIECE), :], xnt_ref[th], preferred_element_type=F32)

    act_unit(0, 0)
    for c in range(2):
        g_prev, g_cur = (gb_s, ga_s) if c == 0 else (ga_s, gb_s)

        def piece(r, carry, c=c, g_prev=g_prev, g_cur=g_cur):
            p = c * per_chunk + r
            rows = pl.ds(pl.multiple_of(r * PEER_PIECE, PEER_PIECE), PEER_PIECE)
            v_rows = vkeep_s[rows, :] if c == 0 else vt_ref[rows, 0:PEER_ECHUNK]
            for th in range(2):
                ts = slice(th * PEER_UNIT_TOKENS, (th + 1) * PEER_UNIT_TOKENS)
                act_unit(p + th, 1 - th)
                acc_s[rows, ts] += jnp.dot(v_rows, g_prev[:, ts], preferred_element_type=F32)
                _gate_unit(p * keys_per_piece, act_s.at[th], need_ref, e1_ref, s2_ref, e2_ref, g_cur,
                           r * PEER_PIECE, th)
            return carry

        lax.fori_loop(0, per_chunk, piece, 0)
    vkeep_s[...] = vt_ref[:, PEER_ECHUNK:2 * PEER_ECHUNK]

    @pl.when(k == pl.num_programs(1) - 1)
    def _():
        acc = acc_s[...] + jnp.dot(vkeep_s[...], gb_s[...], preferred_element_type=F32)
        x2 = x1_ref[...] + acc.T
        y_ref[...] = x2 * lax.rsqrt(jnp.mean(x2 * x2, axis=-1, keepdims=True) + NORM_EPS) * fg_ref[...]


def _peer(xnt, u_bf, vt_bf, need, e1, s2, e2, x1, final_g):
    t = x1.shape[0]
    tt = min(PEER_TOKENS, t)
    assert tt == 2 * PEER_UNIT_TOKENS and need.shape[-1] == PEER_UNIT_TOKENS
    step = 2 * PEER_ECHUNK
    step_keys = step // PEER_NKEYS
    halves = tt // PEER_UNIT_TOKENS
    key_rows = pl.BlockSpec((halves, PEER_HEADS, step_keys, PEER_UNIT_TOKENS), lambda i, e: (i, 0, e, 0))
    all_keys = pl.BlockSpec((halves, PEER_HEADS, PEER_NKEYS, PEER_UNIT_TOKENS), lambda i, e: (i, 0, 0, 0))
    return pl.pallas_call(
        _peer_kernel,
        grid=(t // tt, PEER_EXPERTS // step),
        in_specs=[pl.BlockSpec((halves, D_MODEL, PEER_UNIT_TOKENS), lambda i, e: (i, 0, 0)),
                  pl.BlockSpec((step, D_MODEL), lambda i, e: (e, 0)),
                  pl.BlockSpec((D_MODEL, step), lambda i, e: (0, e)),
                  key_rows, key_rows, all_keys, all_keys,
                  pl.BlockSpec((tt, D_MODEL), lambda i, e: (i, 0)),
                  pl.BlockSpec((1, D_MODEL), lambda i, e: (0, 0))],
        out_specs=pl.BlockSpec((tt, D_MODEL), lambda i, e: (i, 0)),
        out_shape=jax.ShapeDtypeStruct((t, D_MODEL), F32),
        scratch_shapes=[pltpu.VMEM((PEER_ECHUNK, tt), BF16),
                        pltpu.VMEM((PEER_ECHUNK, tt), BF16),
                        pltpu.VMEM((D_MODEL, PEER_ECHUNK), BF16),
                        pltpu.VMEM((D_MODEL, tt), F32),
                        pltpu.VMEM((2, PEER_PIECE, PEER_UNIT_TOKENS), F32)],
        compiler_params=pltpu.CompilerParams(dimension_semantics=("arbitrary",) * 2, vmem_limit_bytes=VMEM_LIMIT),
        name="peer_experts",
    )(xnt, u_bf, vt_bf, need, e1, s2, e2, x1, final_g)


def _alibi_slopes(n):
    start = 2.0 ** (-8.0 / n)
    return jnp.asarray(np.array([start ** (h + 1) for h in range(n)], dtype=np.float32))


def kernel(x, norm1_g, w_in, conv_w, conv_b, b_igate, b_fgate, lam_q1, lam_k1, lam_q2, lam_k2, da_subln_g, ml_norm_g, w_out, norm2_g, w_pq, sub_keys, u_emb, v_emb, final_g):
    bsz, seq, d = x.shape
    t = bsz * seq
    x2 = x.reshape(t, d)
    l = 0
    w_main = w_in[l][:, :N_MAIN].astype(BF16)
    gate_pad = jnp.zeros((d, LANES - ML_HEADS), F32)
    w_gate = jnp.concatenate([w_in[l][:, N_MAIN:N_MAIN + ML_HEADS], gate_pad,
                              w_in[l][:, N_MAIN + ML_HEADS:], gate_pad], axis=1).astype(BF16)
    bias_pad = jnp.zeros((LANES - ML_HEADS,), F32)
    b_gate = jnp.concatenate([b_igate[l], bias_pad, b_fgate[l], bias_pad])[None, :]
    lamv = jnp.stack([lam_q1[l], lam_k1[l], lam_q2[l], lam_k2[l]])

    q0, q1, da_k, da_v, ml_q, ml_k, ml_v, ml_o, gates = _inproj(x2, norm1_g[l][None, :], w_main, w_gate, b_gate)
    seq3 = lambda z: z.reshape(bsz, seq, z.shape[-1])
    a = _diff_attention(seq3(q0), seq3(q1), seq3(da_k), seq3(da_v), _alibi_slopes(DA_HEADS), lamv,
                        da_subln_g[l][None, :])
    hm = _mlstm(seq3(ml_q), seq3(ml_k), seq3(ml_v), seq3(ml_o), seq3(gates),
                conv_w[l], conv_b[l][None, :], ml_norm_g[l][None, :])
    x1, xnt, need, e1, s2, e2 = _mid(a.reshape(t, DA_WIDTH), hm.reshape(t, ML_WIDTH), x2, w_out[l].astype(BF16),
                                     norm2_g[l][None, :], w_pq[l].astype(BF16),
                                     sub_keys[l].reshape(2 * PEER_HEADS, PEER_NKEYS, PEER_HALF).astype(BF16))
    y = _peer(xnt, u_emb[l].astype(BF16), v_emb[l].T.astype(BF16), need, e1, s2, e2, x1, final_g[None, :])
    return y.reshape(bsz, seq, d)
```

```python
import numpy as np
import jax
import jax.numpy as jnp
from jax import lax
from jax.experimental import pallas as pl
from jax.experimental.pallas import tpu as pltpu

F32 = jnp.float32
BF16 = jnp.bfloat16

D_MODEL = 1024
DA_HEADS = 4
DA_QK_DIM = 64
DA_V_DIM = 128
DA_WIDTH = 512
ML_HEADS = 4
ML_HEAD_DIM = 128
ML_WIDTH = 512
ML_CONV = 4
N_MAIN = 7 * 512
PEER_HEADS = 8
PEER_NKEYS = 128
PEER_EXPERTS = PEER_NKEYS * PEER_NKEYS
PEER_HALF = 128
PEER_TOPK = 16
NORM_EPS = 1e-6
SUBLN_EPS = 1e-5
NEG_INF = -1e30
LAM_INIT = 0.2
LANES = 128
SUBLANES = 8

IN_TOKENS = 512
DA_TQ = 256
DA_TK = 512
DA_HEADS_PER_STEP = 2
ML_CHUNK = 256
MID_TOKENS = 512
PEER_TOKENS = 512
PEER_ECHUNK = 1024
PEER_PIECE = 512
PEER_UNIT_TOKENS = 256
CAND_COUNTS = tuple(PEER_TOPK // (i + 1) for i in range(PEER_TOPK))
CAND_OFFSETS = tuple(sum(CAND_COUNTS[:i]) for i in range(PEER_TOPK))
N_CAND = sum(CAND_COUNTS)
N_CAND_PAD = -(-N_CAND // SUBLANES) * SUBLANES
VMEM_LIMIT = 56 * 1024 * 1024


def _nt_dot(a, b):
    return lax.dot_general(a, b, (((1,), (1,)), ((), ())), preferred_element_type=F32)


def _inproj_kernel(x_ref, g_ref, w_ref, wg_ref, bg_ref,
                   q0_ref, q1_ref, dak_ref, dav_ref, mlq_ref, mlk_ref, mlv_ref, mlo_ref, gate_ref):
    x = x_ref[...]
    h = x * lax.rsqrt(jnp.mean(x * x, axis=-1, keepdims=True) + NORM_EPS) * g_ref[...]
    hb = h.astype(BF16)

    def mm(group):
        return jnp.dot(hb, w_ref[:, group * 512:(group + 1) * 512], preferred_element_type=F32)

    q = mm(0) * (DA_QK_DIM ** -0.5)
    lane = lax.broadcasted_iota(jnp.int32, q.shape, 1)
    first = (lane % (2 * DA_QK_DIM)) < DA_QK_DIM
    q0_ref[...] = jnp.where(first, q, 0.0).astype(BF16)
    q1_ref[...] = jnp.where(first, 0.0, q).astype(BF16)
    dak_ref[...] = mm(1).astype(BF16)
    dav_ref[...] = mm(2).astype(BF16)
    mlq_ref[...] = mm(3)
    mlk_ref[...] = mm(4)
    mlv_ref[...] = mm(5).astype(BF16)
    mlo_ref[...] = mm(6)
    gate_ref[...] = jnp.dot(hb, wg_ref[...], preferred_element_type=F32) + bg_ref[...]


def _inproj(x2, norm1_g, w_main, w_gate, b_gate):
    t = x2.shape[0]
    tt = min(IN_TOKENS, t)
    row = lambda i: (i, 0)
    fixed = lambda i: (0, 0)
    wide = lambda dt: jax.ShapeDtypeStruct((t, 512), dt)
    return pl.pallas_call(
        _inproj_kernel,
        grid=(t // tt,),
        in_specs=[pl.BlockSpec((tt, D_MODEL), row),
                  pl.BlockSpec((1, D_MODEL), fixed),
                  pl.BlockSpec((D_MODEL, N_MAIN), fixed),
                  pl.BlockSpec((D_MODEL, 2 * LANES), fixed),
                  pl.BlockSpec((1, 2 * LANES), fixed)],
        out_specs=[pl.BlockSpec((tt, 512), row)] * 8 + [pl.BlockSpec((tt, 2 * LANES), row)],
        out_shape=[wide(BF16), wide(BF16), wide(BF16), wide(BF16), wide(F32), wide(F32), wide(BF16), wide(F32),
                   jax.ShapeDtypeStruct((t, 2 * LANES), F32)],
        compiler_params=pltpu.CompilerParams(dimension_semantics=("arbitrary",), vmem_limit_bytes=VMEM_LIMIT),
        name="inproj",
    )(x2, norm1_g, w_main, w_gate, b_gate)


def _da_kernel(slope_ref, lamv_ref, g_ref, q0_ref, q1_ref, k_ref, v_ref, o_ref, m_s, acc_s, s_s):
    hp = pl.program_id(1)
    i = pl.program_id(2)
    tq = q0_ref.shape[0]
    m_s[...] = jnp.full(m_s.shape, NEG_INF, F32)
    acc_s[...] = jnp.zeros(acc_s.shape, F32)
    one_col = jnp.where(lax.broadcasted_iota(jnp.int32, (DA_TK, LANES), 1) == 0, 1.0, 0.0).astype(BF16)

    def scores(j, slot):
        kstart = pl.multiple_of(j * DA_TK, DA_TK)
        krel = (kstart - i * tq + lax.broadcasted_iota(jnp.int32, (1, DA_TK), 1)).astype(F32)
        for hh in range(DA_HEADS_PER_STEP):
            hs = slice(hh * LANES, (hh + 1) * LANES)
            q2 = jnp.concatenate([q0_ref[:, hs], q1_ref[:, hs]], axis=0)
            s_s[slot, hh] = (_nt_dot(q2, k_ref[pl.ds(kstart, DA_TK), hs])
                             + slope_ref[hp * DA_HEADS_PER_STEP + hh] * krel)

    def softmax_pv(j, slot, masked):
        kstart = pl.multiple_of(j * DA_TK, DA_TK)
        for hh in range(DA_HEADS_PER_STEP):
            hs = slice(hh * LANES, (hh + 1) * LANES)
            v_ext = jnp.concatenate([v_ref[pl.ds(kstart, DA_TK), hs], one_col], axis=1)
            s = s_s[slot, hh]
            if masked:
                row = lax.broadcasted_iota(jnp.int32, s.shape, 0)
                col = lax.broadcasted_iota(jnp.int32, s.shape, 1)
                rq = jnp.where(row >= tq, row - tq, row)
                s = jnp.where(col + (kstart - i * tq) <= rq, s, NEG_INF)
            m_prev = m_s[hh]
            m_new = jnp.maximum(m_prev, jnp.max(s, axis=-1, keepdims=True))
            alpha = jnp.exp(m_prev - m_new)
            p = jnp.exp(s - m_new).astype(BF16)
            acc_s[hh] = alpha * acc_s[hh] + jnp.dot(p, v_ext, preferred_element_type=F32)
            m_s[hh] = m_new

    last = (i * tq) // DA_TK

    def block(j, slot):
        @pl.when(j < last)
        def _():
            scores(j + 1, 1 - slot)
            softmax_pv(j, slot, False)

        @pl.when(j == last)
        def _():
            softmax_pv(j, slot, True)

    def pair(jj, carry):
        block(2 * jj, 0)
        block(2 * jj + 1, 1)
        return carry

    scores(0, 0)
    lax.fori_loop(0, last // 2 + 1, pair, 0)

    lamv = lamv_ref[...]
    lam = (jnp.exp(jnp.sum(lamv[0:1] * lamv[1:2], axis=-1, keepdims=True))
           - jnp.exp(jnp.sum(lamv[2:3] * lamv[3:4], axis=-1, keepdims=True)) + LAM_INIT)
    for hh in range(DA_HEADS_PER_STEP):
        acc = acc_s[hh]
        o = acc[:, 0:DA_V_DIM] / acc[:, DA_V_DIM:DA_V_DIM + 1]
        a = o[:tq] - lam * o[tq:]
        y = a * lax.rsqrt(jnp.mean(a * a, axis=-1, keepdims=True) + SUBLN_EPS) * g_ref[...]
        o_ref[:, hh * LANES:(hh + 1) * LANES] = (y * (1.0 - LAM_INIT)).astype(BF16)


def _diff_attention(q0, q1, k, v, slopes, lamv, subln_g):
    b, s, _ = q0.shape
    assert DA_TK % DA_TQ == 0 and s % DA_TK == 0
    hw = DA_HEADS_PER_STEP * LANES
    qspec = pl.BlockSpec((None, DA_TQ, hw), lambda bi, hi, qi: (bi, qi, hi))
    kvspec = pl.BlockSpec((None, s, hw), lambda bi, hi, qi: (bi, 0, hi))
    return pl.pallas_call(
        _da_kernel,
        grid=(b, DA_HEADS // DA_HEADS_PER_STEP, s // DA_TQ),
        in_specs=[pl.BlockSpec(memory_space=pltpu.SMEM),
                  pl.BlockSpec((4, DA_QK_DIM), lambda bi, hi, qi: (0, 0)),
                  pl.BlockSpec((1, DA_V_DIM), lambda bi, hi, qi: (0, 0)),
                  qspec, qspec, kvspec, kvspec],
        out_specs=qspec,
        out_shape=jax.ShapeDtypeStruct((b, s, DA_WIDTH), BF16),
        scratch_shapes=[pltpu.VMEM((DA_HEADS_PER_STEP, 2 * DA_TQ, 1), F32),
                        pltpu.VMEM((DA_HEADS_PER_STEP, 2 * DA_TQ, 2 * DA_V_DIM), F32),
                        pltpu.VMEM((2, DA_HEADS_PER_STEP, 2 * DA_TQ, DA_TK), F32)],
        compiler_params=pltpu.CompilerParams(dimension_semantics=("arbitrary",) * 3, vmem_limit_bytes=VMEM_LIMIT),
        name="diff_attention",
    )(slopes, lamv, subln_g, q0, q1, k, v)


def _mlstm_kernel(tril_ref, cw_ref, cb_ref, ng_ref, q_ref, k_ref, v_ref, og_ref, gate_ref, out_ref,
                  xbuf, cn_s, m_s):
    c = pl.program_id(1)
    L = q_ref.shape[0]

    @pl.when(c == 0)
    def _():
        xbuf[0:SUBLANES, :] = jnp.zeros((SUBLANES, 2 * ML_WIDTH), F32)
        cn_s[...] = jnp.zeros(cn_s.shape, F32)
        m_s[...] = jnp.full(m_s.shape, NEG_INF, F32)

    @pl.when(c > 0)
    def _():
        xbuf[0:SUBLANES, :] = xbuf[L:L + SUBLANES, :]

    xbuf[SUBLANES:SUBLANES + L, 0:ML_WIDTH] = q_ref[...]
    xbuf[SUBLANES:SUBLANES + L, ML_WIDTH:2 * ML_WIDTH] = k_ref[...]
    y = cb_ref[...] + cw_ref[0:1, :] * xbuf[SUBLANES - 3:SUBLANES - 3 + L, :]
    for j in range(1, ML_CONV):
        y = y + cw_ref[j:j + 1, :] * xbuf[SUBLANES - 3 + j:SUBLANES - 3 + j + L, :]
    qk = y * jax.nn.sigmoid(y)

    gates = gate_ref[...]
    i_log = gates[:, 0:LANES]
    f_log = jax.nn.log_sigmoid(gates[:, LANES:2 * LANES])
    b_all = jnp.dot(tril_ref[...], f_log, preferred_element_type=F32,
                    precision=lax.Precision.HIGHEST)
    r_all = i_log - b_all
    r_all_t = r_all.T
    row = lax.broadcasted_iota(jnp.int32, (L, L), 0)
    col = lax.broadcasted_iota(jnp.int32, (L, L), 1)
    tri = col <= row
    one_col = jnp.where(lax.broadcasted_iota(jnp.int32, (L, LANES), 1) == 0, 1.0, 0.0).astype(BF16)

    for h in range(ML_HEADS):
        hs = slice(h * ML_HEAD_DIM, (h + 1) * ML_HEAD_DIM)
        b_col = b_all[:, h:h + 1]
        r_col = r_all[:, h:h + 1]
        r_row = r_all_t[h:h + 1, :]
        m_prev = m_s[h:h + 1, 0:1]
        d = jnp.where(tri, b_col + r_row, NEG_INF)
        inter = b_col + m_prev
        mt = jnp.maximum(inter, jnp.max(d, axis=-1, keepdims=True))
        w_intra = jnp.exp(d - mt)
        w_inter = jnp.exp(inter - mt)
        qb = qk[:, hs].astype(BF16)
        kf = qk[:, ML_WIDTH + h * ML_HEAD_DIM:ML_WIDTH + (h + 1) * ML_HEAD_DIM] * (ML_HEAD_DIM ** -0.5)
        s = _nt_dot(qb, kf.astype(BF16)) * w_intra
        v_ext = jnp.concatenate([v_ref[:, hs], one_col], axis=1)
        cn = cn_s[h]
        num_ext = (w_inter * jnp.dot(qb, cn.astype(BF16), preferred_element_type=F32)
                   + jnp.dot(s.astype(BF16), v_ext, preferred_element_type=F32))
        num = num_ext[:, 0:ML_HEAD_DIM]
        den = num_ext[:, ML_HEAD_DIM:ML_HEAD_DIM + 1]
        hh = num / jnp.maximum(jnp.abs(den), jnp.exp(-mt))
        m_new = mt[L - 1:L, :]
        b_last = b_col[L - 1:L, :]
        ws = jnp.exp(b_last + r_col - m_new)
        decay = jnp.exp(b_last + m_prev - m_new)
        kw_t = (kf * ws).T.astype(BF16)
        cn_s[h] = decay * cn + jnp.dot(kw_t, v_ext, preferred_element_type=F32)
        m_s[h:h + 1, :] = jnp.broadcast_to(m_new, (1, LANES))
        hn = hh * lax.rsqrt(jnp.mean(hh * hh, axis=-1, keepdims=True) + NORM_EPS) * ng_ref[:, hs]
        out_ref[:, hs] = (hn * jax.nn.sigmoid(og_ref[:, hs])).astype(BF16)


def _mlstm(ml_q, ml_k, ml_v, ml_o, gates, conv_w, conv_b, norm_g):
    b, s, _ = ml_q.shape
    L = min(ML_CHUNK, s)
    tril = jnp.tril(jnp.ones((L, L), F32))
    fixed = lambda bi, ci: (0, 0)
    seq = lambda w: pl.BlockSpec((None, L, w), lambda bi, ci: (bi, ci, 0))
    return pl.pallas_call(
        _mlstm_kernel,
        grid=(b, s // L),
        in_specs=[pl.BlockSpec((L, L), fixed),
                  pl.BlockSpec((ML_CONV, 2 * ML_WIDTH), fixed),
                  pl.BlockSpec((1, 2 * ML_WIDTH), fixed),
                  pl.BlockSpec((1, ML_WIDTH), fixed),
                  seq(ML_WIDTH), seq(ML_WIDTH), seq(ML_WIDTH), seq(ML_WIDTH), seq(2 * LANES)],
        out_specs=seq(ML_WIDTH),
        out_shape=jax.ShapeDtypeStruct((b, s, ML_WIDTH), BF16),
        scratch_shapes=[pltpu.VMEM((L + 2 * SUBLANES, 2 * ML_WIDTH), F32),
                        pltpu.VMEM((ML_HEADS, ML_HEAD_DIM, 2 * ML_HEAD_DIM), F32),
                        pltpu.VMEM((SUBLANES, LANES), F32)],
        compiler_params=pltpu.CompilerParams(dimension_semantics=("arbitrary",) * 2, vmem_limit_bytes=VMEM_LIMIT),
        name="mlstm",
    )(tril, conv_w, conv_b, norm_g, ml_q, ml_k, ml_v, ml_o, gates)


def _extract_top_exact(s, n, out_ref, lead):
    rows = s.shape[0]
    iota = lax.broadcasted_iota(jnp.int32, s.shape, 0)
    for k in range(n):
        mx = jnp.max(s, axis=0, keepdims=True)
        out_ref[lead, k:k + 1, :] = mx
        first = jnp.min(jnp.where(s == mx, iota, rows), axis=0, keepdims=True)
        s = jnp.where(iota == first, -jnp.inf, s)
    return s


def _extract_top_fast(s, n, out_ref, lead, n_pad=0):
    for k in range(n):
        mx = jnp.max(s, axis=0, keepdims=True)
        out_ref[lead, k:k + 1, :] = mx
        s = jnp.where(s == mx, -jnp.inf, s)
    removed = jnp.sum(jnp.where(s == -jnp.inf, 1.0, 0.0), axis=0, keepdims=True)
    return s, removed - float(n + n_pad)


def _mid_kernel(a_ref, hm_ref, x_ref, wout_ref, g2_ref, wpq_ref, sk_ref,
                x1_ref, xnt_ref, need_ref, e1_ref, s2_ref, e2_ref, q_s, vals_s, cand_s):
    x1 = (x_ref[...]
          + jnp.dot(a_ref[...], wout_ref[0:DA_WIDTH, :], preferred_element_type=F32)
          + jnp.dot(hm_ref[...], wout_ref[DA_WIDTH:DA_WIDTH + ML_WIDTH, :], preferred_element_type=F32))
    x1_ref[...] = x1
    xn = x1 * lax.rsqrt(jnp.mean(x1 * x1, axis=-1, keepdims=True) + NORM_EPS) * g2_ref[...]
    xnt = xn.T.astype(BF16)
    tiles = [slice(i * PEER_UNIT_TOKENS, (i + 1) * PEER_UNIT_TOKENS) for i in range(xnt.shape[1] // PEER_UNIT_TOKENS)]
    for i, ts in enumerate(tiles):
        xnt_ref[i] = xnt[:, ts]
    q = jnp.dot(xn.astype(BF16), wpq_ref[...], preferred_element_type=F32).astype(BF16)
    for hc in range(2 * PEER_HEADS):
        q_s[hc] = q[:, hc * PEER_HALF:(hc + 1) * PEER_HALF]
    tt = x1.shape[0]
    cand_s[N_CAND:N_CAND_PAD, :] = jnp.full((N_CAND_PAD - N_CAND, tt), -jnp.inf, F32)

    def head(h, carry):
        s1 = _nt_dot(sk_ref[2 * h], q_s[2 * h])
        s2 = _nt_dot(sk_ref[2 * h + 1], q_s[2 * h + 1])

        def select(extract):
            r1 = extract(s1, 0)
            r2 = extract(s2, 1)
            for i, cnt in enumerate(CAND_COUNTS):
                off = CAND_OFFSETS[i]
                cand_s[off:off + cnt, :] = vals_s[0, i:i + 1, :] + vals_s[1, 0:cnt, :]
            r3 = extract(cand_s[...], 2, N_CAND_PAD - N_CAND)
            cand_s[...] = r3[0]
            return jnp.maximum(jnp.maximum(r1[1], r2[1]), r3[1])

        def fast(s, lead, n_pad=0):
            return _extract_top_fast(s, PEER_TOPK, vals_s, lead, n_pad)

        def exact(s, lead, n_pad=0):
            return _extract_top_exact(s, PEER_TOPK, vals_s, lead), jnp.zeros((1, tt), F32)

        excess = select(fast)

        @pl.when(jnp.max(excess) > 0.0)
        def _():
            select(exact)

        top = vals_s[2, 0:PEER_TOPK, :]
        t0 = top[0:1, :]
        lse = t0 + jnp.log(jnp.sum(jnp.exp(top - t0), axis=0, keepdims=True))
        need = jnp.full(s1.shape, jnp.inf, F32)
        for i in reversed(range(PEER_TOPK)):
            cnt, off = CAND_COUNTS[i], CAND_OFFSETS[i]
            cut = jnp.min(jnp.where(cand_s[off:off + cnt, :] == -jnp.inf, vals_s[1, 0:cnt, :], jnp.inf),
                          axis=0, keepdims=True)
            need = jnp.where(s1 >= vals_s[0, i:i + 1, :], cut, need)
        c1 = vals_s[0, 0:1, :]
        e1 = jnp.exp(s1 - c1)
        e2 = 0.5 * jnp.exp(s2 - (lse - c1))
        for i, ts in enumerate(tiles):
            need_ref[i, h] = need[:, ts]
            e1_ref[i, h] = e1[:, ts]
            s2_ref[i, h] = s2[:, ts]
            e2_ref[i, h] = e2[:, ts]
        return carry

    lax.fori_loop(0, PEER_HEADS, head, 0)


def _mid(a, hm, x2, w_out, norm2_g, w_pq, sub_keys):
    t = x2.shape[0]
    tt = min(MID_TOKENS, t)
    tile = PEER_UNIT_TOKENS
    n_tile = tt // tile
    row = lambda i: (i, 0)
    fixed = lambda i: (0, 0)
    return pl.pallas_call(
        _mid_kernel,
        grid=(t // tt,),
        in_specs=[pl.BlockSpec((tt, DA_WIDTH), row),
                  pl.BlockSpec((tt, ML_WIDTH), row),
                  pl.BlockSpec((tt, D_MODEL), row),
                  pl.BlockSpec((D_MODEL, D_MODEL), fixed),
                  pl.BlockSpec((1, D_MODEL), fixed),
                  pl.BlockSpec((D_MODEL, 2 * PEER_HEADS * PEER_HALF), fixed),
                  pl.BlockSpec((2 * PEER_HEADS, PEER_NKEYS, PEER_HALF), lambda i: (0, 0, 0))],
        out_specs=[pl.BlockSpec((tt, D_MODEL), row),
                   pl.BlockSpec((n_tile, D_MODEL, tile), lambda i: (i, 0, 0))]
                  + [pl.BlockSpec((n_tile, PEER_HEADS, PEER_NKEYS, tile), lambda i: (i, 0, 0, 0))] * 4,
        out_shape=[jax.ShapeDtypeStruct((t, D_MODEL), F32),
                   jax.ShapeDtypeStruct((t // tile, D_MODEL, tile), BF16)]
                  + [jax.ShapeDtypeStruct((t // tile, PEER_HEADS, PEER_NKEYS, tile), F32)] * 4,
        scratch_shapes=[pltpu.VMEM((2 * PEER_HEADS, tt, PEER_HALF), BF16),
                        pltpu.VMEM((3, 2 * SUBLANES, tt), F32),
                        pltpu.VMEM((N_CAND_PAD, tt), F32)],
        compiler_params=pltpu.CompilerParams(dimension_semantics=("arbitrary",), vmem_limit_bytes=VMEM_LIMIT),
        name="mid",
    )(a, hm, x2, w_out, norm2_g, w_pq, sub_keys)


def _gate_unit(key0, act_ref, need_ref, e1_ref, s2_ref, e2_ref, g_ref, row0, th):
    for q in range(PEER_PIECE // PEER_NKEYS):
        need_rows = [need_ref[th, h, pl.ds(key0 + q, 1), :] for h in range(PEER_HEADS)]
        e1_rows = [e1_ref[th, h, pl.ds(key0 + q, 1), :] for h in range(PEER_HEADS)]
        for lt in range(PEER_UNIT_TOKENS // LANES):
            la = slice(lt * LANES, (lt + 1) * LANES)
            ls = slice(th * PEER_UNIT_TOKENS + lt * LANES, th * PEER_UNIT_TOKENS + (lt + 1) * LANES)
            w = None
            for h in range(PEER_HEADS):
                wh = jnp.where(s2_ref[th, h, :, la] >= need_rows[h][:, la], e2_ref[th, h, :, la], 0.0) * e1_rows[h][:, la]
                w = wh if w is None else w + wh
            a = act_ref[q * PEER_NKEYS:(q + 1) * PEER_NKEYS, la]
            g_ref[pl.ds(pl.multiple_of(row0 + q * PEER_NKEYS, PEER_NKEYS), PEER_NKEYS), ls] = (
                a * (1.0 + lax.erf(a * np.float32(np.sqrt(0.5)))) * w).astype(BF16)


def _peer_kernel(xnt_ref, u_ref, u_next_ref, vt_ref, need_ref, e1_ref, s2_ref, e2_ref, x1_ref, fg_ref, y_ref,
                 ga_s, gb_s, vkeep_s, acc_s, act_s):
    k = pl.program_id(1)
    per_chunk = PEER_ECHUNK // PEER_PIECE
    n_piece = 2 * per_chunk
    keys_per_piece = PEER_PIECE // PEER_NKEYS

    def act_unit(u_rows, th):
        act_s[th] = jnp.dot(u_rows, xnt_ref[th], preferred_element_type=F32)

    def u_piece(p):
        return u_ref[pl.ds(pl.multiple_of(p * PEER_PIECE, PEER_PIECE), PEER_PIECE), :]

    @pl.when(k == 0)
    def _():
        acc_s[...] = jnp.zeros(acc_s.shape, F32)
        gb_s[...] = jnp.zeros(gb_s.shape, BF16)
        vkeep_s[...] = jnp.zeros(vkeep_s.shape, BF16)
        act_unit(u_piece(0), 0)

    def piece(r, c, last):
        g_prev, g_cur = (gb_s, ga_s) if c == 0 else (ga_s, gb_s)
        p = c * per_chunk + r
        rows = pl.ds(pl.multiple_of(r * PEER_PIECE, PEER_PIECE), PEER_PIECE)
        v_rows = vkeep_s[rows, :] if c == 0 else vt_ref[rows, 0:PEER_ECHUNK]
        for th in range(2):
            ts = slice(th * PEER_UNIT_TOKENS, (th + 1) * PEER_UNIT_TOKENS)
            if th == 0:
                act_unit(u_piece(p), 1)
            else:
                act_unit(u_next_ref[...] if last else u_piece(p + 1), 0)
            acc_s[rows, ts] += jnp.dot(v_rows, g_prev[:, ts], preferred_element_type=F32)
            _gate_unit(p * keys_per_piece, act_s.at[th], need_ref, e1_ref, s2_ref, e2_ref, g_cur,
                       r * PEER_PIECE, th)

    def rolled(c, n):
        def body(r, carry):
            piece(r, c, False)
            return carry
        lax.fori_loop(0, jnp.minimum(k + n, n), body, 0)

    rolled(0, per_chunk)
    rolled(1, per_chunk - 1)
    piece(per_chunk - 1, 1, True)
    vkeep_s[...] = vt_ref[:, PEER_ECHUNK:2 * PEER_ECHUNK]

    @pl.when(k == pl.num_programs(1) - 1)
    def _():
        acc = acc_s[...] + jnp.dot(vkeep_s[...], gb_s[...], preferred_element_type=F32)
        x2 = x1_ref[...] + acc.T
        y_ref[...] = x2 * lax.rsqrt(jnp.mean(x2 * x2, axis=-1, keepdims=True) + NORM_EPS) * fg_ref[...]


def _peer(xnt, u_bf, vt_bf, need, e1, s2, e2, x1, final_g):
    t = x1.shape[0]
    tt = min(PEER_TOKENS, t)
    assert tt == 2 * PEER_UNIT_TOKENS and need.shape[-1] == PEER_UNIT_TOKENS
    step = 2 * PEER_ECHUNK
    n_step = PEER_EXPERTS // step
    step_keys = step // PEER_NKEYS
    halves = tt // PEER_UNIT_TOKENS
    key_rows = pl.BlockSpec((halves, PEER_HEADS, step_keys, PEER_UNIT_TOKENS), lambda i, e: (i, 0, e, 0))
    all_keys = pl.BlockSpec((halves, PEER_HEADS, PEER_NKEYS, PEER_UNIT_TOKENS), lambda i, e: (i, 0, 0, 0))
    return pl.pallas_call(
        _peer_kernel,
        grid=(t // tt, n_step),
        in_specs=[pl.BlockSpec((halves, D_MODEL, PEER_UNIT_TOKENS), lambda i, e: (i, 0, 0)),
                  pl.BlockSpec((step, D_MODEL), lambda i, e: (e, 0)),
                  pl.BlockSpec((PEER_PIECE, D_MODEL),
                               lambda i, e: (jnp.minimum(e + 1, n_step - 1) * (step // PEER_PIECE), 0)),
                  pl.BlockSpec((D_MODEL, step), lambda i, e: (0, e)),
                  key_rows, key_rows, all_keys, all_keys,
                  pl.BlockSpec((tt, D_MODEL), lambda i, e: (i, 0)),
                  pl.BlockSpec((1, D_MODEL), lambda i, e: (0, 0))],
        out_specs=pl.BlockSpec((tt, D_MODEL), lambda i, e: (i, 0)),
        out_shape=jax.ShapeDtypeStruct((t, D_MODEL), F32),
        scratch_shapes=[pltpu.VMEM((PEER_ECHUNK, tt), BF16),
                        pltpu.VMEM((PEER_ECHUNK, tt), BF16),
                        pltpu.VMEM((D_MODEL, PEER_ECHUNK), BF16),
                        pltpu.VMEM((D_MODEL, tt), F32),
                        pltpu.VMEM((2, PEER_PIECE, PEER_UNIT_TOKENS), F32)],
        compiler_params=pltpu.CompilerParams(dimension_semantics=("arbitrary",) * 2, vmem_limit_bytes=VMEM_LIMIT),
        name="peer_experts",
    )(xnt, u_bf, u_bf, vt_bf, need, e1, s2, e2, x1, final_g)


def _alibi_slopes(n):
    start = 2.0 ** (-8.0 / n)
    return jnp.asarray(np.array([start ** (h + 1) for h in range(n)], dtype=np.float32))


def kernel(x, norm1_g, w_in, conv_w, conv_b, b_igate, b_fgate, lam_q1, lam_k1, lam_q2, lam_k2, da_subln_g, ml_norm_g, w_out, norm2_g, w_pq, sub_keys, u_emb, v_emb, final_g):
    bsz, seq, d = x.shape
    t = bsz * seq
    x2 = x.reshape(t, d)
    l = 0
    w_main = w_in[l][:, :N_MAIN].astype(BF16)
    gate_pad = jnp.zeros((d, LANES - ML_HEADS), F32)
    w_gate = jnp.concatenate([w_in[l][:, N_MAIN:N_MAIN + ML_HEADS], gate_pad,
                              w_in[l][:, N_MAIN + ML_HEADS:], gate_pad], axis=1).astype(BF16)
    bias_pad = jnp.zeros((LANES - ML_HEADS,), F32)
    b_gate = jnp.concatenate([b_igate[l], bias_pad, b_fgate[l], bias_pad])[None, :]
    lamv = jnp.stack([lam_q1[l], lam_k1[l], lam_q2[l], lam_k2[l]])

    q0, q1, da_k, da_v, ml_q, ml_k, ml_v, ml_o, gates = _inproj(x2, norm1_g[l][None, :], w_main, w_gate, b_gate)
    seq3 = lambda z: z.reshape(bsz, seq, z.shape[-1])
    a = _diff_attention(seq3(q0), seq3(q1), seq3(da_k), seq3(da_v), _alibi_slopes(DA_HEADS), lamv,
                        da_subln_g[l][None, :])
    hm = _mlstm(seq3(ml_q), seq3(ml_k), seq3(ml_v), seq3(ml_o), seq3(gates),
                conv_w[l], conv_b[l][None, :], ml_norm_g[l][None, :])
    x1, xnt, need, e1, s2, e2 = _mid(a.reshape(t, DA_WIDTH), hm.reshape(t, ML_WIDTH), x2, w_out[l].astype(BF16),
                                     norm2_g[l][None, :], w_pq[l].astype(BF16),
                                     sub_keys[l].reshape(2 * PEER_HEADS, PEER_NKEYS, PEER_HALF).astype(BF16))
    y = _peer(xnt, u_emb[l].astype(BF16), v_emb[l].T.astype(BF16), need, e1, s2, e2, x1, final_g[None, :])
    return y.reshape(bsz, seq, d)
```

```python
import numpy as np
import jax
import jax.numpy as jnp
from jax import lax
from jax.experimental import pallas as pl
from jax.experimental.pallas import tpu as pltpu

F32 = jnp.float32
BF16 = jnp.bfloat16

D_MODEL = 1024
DA_HEADS = 4
DA_QK_DIM = 64
DA_V_DIM = 128
DA_WIDTH = 512
ML_HEADS = 4
ML_HEAD_DIM = 128
ML_WIDTH = 512
ML_CONV = 4
N_MAIN = 7 * 512
PEER_HEADS = 8
PEER_NKEYS = 128
PEER_EXPERTS = PEER_NKEYS * PEER_NKEYS
PEER_HALF = 128
PEER_TOPK = 16
NORM_EPS = 1e-6
SUBLN_EPS = 1e-5
NEG_INF = -1e30
LAM_INIT = 0.2
LANES = 128
SUBLANES = 8

IN_TOKENS = 512
DA_TQ = 256
DA_TK = 512
DA_HEADS_PER_STEP = 2
ML_CHUNK = 256
MID_TOKENS = 512
PEER_TOKENS = 512
PEER_ECHUNK = 1024
PEER_PIECE = 512
PEER_UNIT_TOKENS = 256
CAND_COUNTS = tuple(PEER_TOPK // (i + 1) for i in range(PEER_TOPK))
CAND_OFFSETS = tuple(sum(CAND_COUNTS[:i]) for i in range(PEER_TOPK))
N_CAND = sum(CAND_COUNTS)
N_CAND_PAD = -(-N_CAND // SUBLANES) * SUBLANES
VMEM_LIMIT = 56 * 1024 * 1024


def _nt_dot(a, b):
    return lax.dot_general(a, b, (((1,), (1,)), ((), ())), preferred_element_type=F32)


def _inproj_kernel(x_ref, g_ref, w_ref, wg_ref, bg_ref,
                   q0_ref, q1_ref, dak_ref, dav_ref, mlq_ref, mlk_ref, mlv_ref, mlo_ref, gate_ref):
    x = x_ref[...]
    h = x * lax.rsqrt(jnp.mean(x * x, axis=-1, keepdims=True) + NORM_EPS) * g_ref[...]
    hb = h.astype(BF16)

    def mm(group):
        return jnp.dot(hb, w_ref[:, group * 512:(group + 1) * 512], preferred_element_type=F32)

    q = mm(0) * (DA_QK_DIM ** -0.5)
    lane = lax.broadcasted_iota(jnp.int32, q.shape, 1)
    first = (lane % (2 * DA_QK_DIM)) < DA_QK_DIM
    q0_ref[...] = jnp.where(first, q, 0.0).astype(BF16)
    q1_ref[...] = jnp.where(first, 0.0, q).astype(BF16)
    dak_ref[...] = mm(1).astype(BF16)
    dav_ref[...] = mm(2).astype(BF16)
    mlq_ref[...] = mm(3)
    mlk_ref[...] = mm(4)
    mlv_ref[...] = mm(5).astype(BF16)
    mlo_ref[...] = mm(6)
    gate_ref[...] = jnp.dot(hb, wg_ref[...], preferred_element_type=F32) + bg_ref[...]


def _inproj(x2, norm1_g, w_main, w_gate, b_gate):
    t = x2.shape[0]
    tt = min(IN_TOKENS, t)
    row = lambda i: (i, 0)
    fixed = lambda i: (0, 0)
    wide = lambda dt: jax.ShapeDtypeStruct((t, 512), dt)
    return pl.pallas_call(
        _inproj_kernel,
        grid=(t // tt,),
        in_specs=[pl.BlockSpec((tt, D_MODEL), row),
                  pl.BlockSpec((1, D_MODEL), fixed),
                  pl.BlockSpec((D_MODEL, N_MAIN), fixed),
                  pl.BlockSpec((D_MODEL, 2 * LANES), fixed),
                  pl.BlockSpec((1, 2 * LANES), fixed)],
        out_specs=[pl.BlockSpec((tt, 512), row)] * 8 + [pl.BlockSpec((tt, 2 * LANES), row)],
        out_shape=[wide(BF16), wide(BF16), wide(BF16), wide(BF16), wide(F32), wide(F32), wide(BF16), wide(F32),
                   jax.ShapeDtypeStruct((t, 2 * LANES), F32)],
        compiler_params=pltpu.CompilerParams(dimension_semantics=("arbitrary",), vmem_limit_bytes=VMEM_LIMIT),
        name="inproj",
    )(x2, norm1_g, w_main, w_gate, b_gate)


def _da_kernel(slope_ref, lamv_ref, g_ref, q0_ref, q1_ref, k_ref, v_ref, o_ref, m_s, acc_s, s_s):
    hp = pl.program_id(1)
    i = pl.program_id(2)
    tq = q0_ref.shape[0]
    m_s[...] = jnp.full(m_s.shape, NEG_INF, F32)
    acc_s[...] = jnp.zeros(acc_s.shape, F32)
    one_col = jnp.where(lax.broadcasted_iota(jnp.int32, (DA_TK, LANES), 1) == 0, 1.0, 0.0).astype(BF16)

    def scores(j, slot):
        kstart = pl.multiple_of(j * DA_TK, DA_TK)
        krel = (kstart - i * tq + lax.broadcasted_iota(jnp.int32, (1, DA_TK), 1)).astype(F32)
        for hh in range(DA_HEADS_PER_STEP):
            hs = slice(hh * LANES, (hh + 1) * LANES)
            q2 = jnp.concatenate([q0_ref[:, hs], q1_ref[:, hs]], axis=0)
            s_s[slot, hh] = (_nt_dot(q2, k_ref[pl.ds(kstart, DA_TK), hs])
                             + slope_ref[hp * DA_HEADS_PER_STEP + hh] * krel)

    def softmax_pv(j, slot, masked):
        kstart = pl.multiple_of(j * DA_TK, DA_TK)
        for hh in range(DA_HEADS_PER_STEP):
            hs = slice(hh * LANES, (hh + 1) * LANES)
            v_ext = jnp.concatenate([v_ref[pl.ds(kstart, DA_TK), hs], one_col], axis=1)
            s = s_s[slot, hh]
            if masked:
                row = lax.broadcasted_iota(jnp.int32, s.shape, 0)
                col = lax.broadcasted_iota(jnp.int32, s.shape, 1)
                rq = jnp.where(row >= tq, row - tq, row)
                s = jnp.where(col + (kstart - i * tq) <= rq, s, NEG_INF)
            m_prev = m_s[hh]
            m_new = jnp.maximum(m_prev, jnp.max(s, axis=-1, keepdims=True))
            alpha = jnp.exp(m_prev - m_new)
            p = jnp.exp(s - m_new).astype(BF16)
            acc_s[hh] = alpha * acc_s[hh] + jnp.dot(p, v_ext, preferred_element_type=F32)
            m_s[hh] = m_new

    last = (i * tq) // DA_TK

    def block(j, slot):
        @pl.when(j < last)
        def _():
            scores(j + 1, 1 - slot)
            softmax_pv(j, slot, False)

        @pl.when(j == last)
        def _():
            softmax_pv(j, slot, True)

    def pair(jj, carry):
        block(2 * jj, 0)
        block(2 * jj + 1, 1)
        return carry

    scores(0, 0)
    lax.fori_loop(0, last // 2 + 1, pair, 0)

    lamv = lamv_ref[...]
    lam = (jnp.exp(jnp.sum(lamv[0:1] * lamv[1:2], axis=-1, keepdims=True))
           - jnp.exp(jnp.sum(lamv[2:3] * lamv[3:4], axis=-1, keepdims=True)) + LAM_INIT)
    for hh in range(DA_HEADS_PER_STEP):
        acc = acc_s[hh]
        o = acc[:, 0:DA_V_DIM] / acc[:, DA_V_DIM:DA_V_DIM + 1]
        a = o[:tq] - lam * o[tq:]
        y = a * lax.rsqrt(jnp.mean(a * a, axis=-1, keepdims=True) + SUBLN_EPS) * g_ref[...]
        o_ref[:, hh * LANES:(hh + 1) * LANES] = (y * (1.0 - LAM_INIT)).astype(BF16)


def _diff_attention(q0, q1, k, v, slopes, lamv, subln_g):
    b, s, _ = q0.shape
    assert DA_TK % DA_TQ == 0 and s % DA_TK == 0
    hw = DA_HEADS_PER_STEP * LANES
    qspec = pl.BlockSpec((None, DA_TQ, hw), lambda bi, hi, qi: (bi, qi, hi))
    kvspec = pl.BlockSpec((None, s, hw), lambda bi, hi, qi: (bi, 0, hi))
    return pl.pallas_call(
        _da_kernel,
        grid=(b, DA_HEADS // DA_HEADS_PER_STEP, s // DA_TQ),
        in_specs=[pl.BlockSpec(memory_space=pltpu.SMEM),
                  pl.BlockSpec((4, DA_QK_DIM), lambda bi, hi, qi: (0, 0)),
                  pl.BlockSpec((1, DA_V_DIM), lambda bi, hi, qi: (0, 0)),
                  qspec, qspec, kvspec, kvspec],
        out_specs=qspec,
        out_shape=jax.ShapeDtypeStruct((b, s, DA_WIDTH), BF16),
        scratch_shapes=[pltpu.VMEM((DA_HEADS_PER_STEP, 2 * DA_TQ, 1), F32),
                        pltpu.VMEM((DA_HEADS_PER_STEP, 2 * DA_TQ, 2 * DA_V_DIM), F32),
                        pltpu.VMEM((2, DA_HEADS_PER_STEP, 2 * DA_TQ, DA_TK), F32)],
        compiler_params=pltpu.CompilerParams(dimension_semantics=("arbitrary",) * 3, vmem_limit_bytes=VMEM_LIMIT),
        name="diff_attention",
    )(slopes, lamv, subln_g, q0, q1, k, v)


def _mlstm_kernel(tril_ref, cw_ref, cb_ref, ng_ref, q_ref, k_ref, v_ref, og_ref, gate_ref, out_ref,
                  xbuf, cn_s, m_s):
    c = pl.program_id(1)
    L = q_ref.shape[0]

    @pl.when(c == 0)
    def _():
        xbuf[0:SUBLANES, :] = jnp.zeros((SUBLANES, 2 * ML_WIDTH), F32)
        cn_s[...] = jnp.zeros(cn_s.shape, F32)
        m_s[...] = jnp.full(m_s.shape, NEG_INF, F32)

    @pl.when(c > 0)
    def _():
        xbuf[0:SUBLANES, :] = xbuf[L:L + SUBLANES, :]

    xbuf[SUBLANES:SUBLANES + L, 0:ML_WIDTH] = q_ref[...]
    xbuf[SUBLANES:SUBLANES + L, ML_WIDTH:2 * ML_WIDTH] = k_ref[...]
    y = cb_ref[...] + cw_ref[0:1, :] * xbuf[SUBLANES - 3:SUBLANES - 3 + L, :]
    for j in range(1, ML_CONV):
        y = y + cw_ref[j:j + 1, :] * xbuf[SUBLANES - 3 + j:SUBLANES - 3 + j + L, :]
    qk = y * jax.nn.sigmoid(y)

    gates = gate_ref[...]
    i_log = gates[:, 0:LANES]
    f_log = jax.nn.log_sigmoid(gates[:, LANES:2 * LANES])
    b_all = jnp.dot(tril_ref[...], f_log, preferred_element_type=F32,
                    precision=lax.Precision.HIGHEST)
    r_all = i_log - b_all
    r_all_t = r_all.T
    row = lax.broadcasted_iota(jnp.int32, (L, L), 0)
    col = lax.broadcasted_iota(jnp.int32, (L, L), 1)
    tri = col <= row
    one_col = jnp.where(lax.broadcasted_iota(jnp.int32, (L, LANES), 1) == 0, 1.0, 0.0).astype(BF16)

    for h in range(ML_HEADS):
        hs = slice(h * ML_HEAD_DIM, (h + 1) * ML_HEAD_DIM)
        b_col = b_all[:, h:h + 1]
        r_col = r_all[:, h:h + 1]
        r_row = r_all_t[h:h + 1, :]
        m_prev = m_s[h:h + 1, 0:1]
        d = jnp.where(tri, b_col + r_row, NEG_INF)
        inter = b_col + m_prev
        mt = jnp.maximum(inter, jnp.max(d, axis=-1, keepdims=True))
        w_intra = jnp.exp(d - mt)
        w_inter = jnp.exp(inter - mt)
        qb = qk[:, hs].astype(BF16)
        kf = qk[:, ML_WIDTH + h * ML_HEAD_DIM:ML_WIDTH + (h + 1) * ML_HEAD_DIM] * (ML_HEAD_DIM ** -0.5)
        s = _nt_dot(qb, kf.astype(BF16)) * w_intra
        v_ext = jnp.concatenate([v_ref[:, hs], one_col], axis=1)
        cn = cn_s[h]
        num_ext = (w_inter * jnp.dot(qb, cn.astype(BF16), preferred_element_type=F32)
                   + jnp.dot(s.astype(BF16), v_ext, preferred_element_type=F32))
        num = num_ext[:, 0:ML_HEAD_DIM]
        den = num_ext[:, ML_HEAD_DIM:ML_HEAD_DIM + 1]
        hh = num / jnp.maximum(jnp.abs(den), jnp.exp(-mt))
        m_new = mt[L - 1:L, :]
        b_last = b_col[L - 1:L, :]
        ws = jnp.exp(b_last + r_col - m_new)
        decay = jnp.exp(b_last + m_prev - m_new)
        kw_t = (kf * ws).T.astype(BF16)
        cn_s[h] = decay * cn + jnp.dot(kw_t, v_ext, preferred_element_type=F32)
        m_s[h:h + 1, :] = jnp.broadcast_to(m_new, (1, LANES))
        hn = hh * lax.rsqrt(jnp.mean(hh * hh, axis=-1, keepdims=True) + NORM_EPS) * ng_ref[:, hs]
        out_ref[:, hs] = (hn * jax.nn.sigmoid(og_ref[:, hs])).astype(BF16)


def _mlstm(ml_q, ml_k, ml_v, ml_o, gates, conv_w, conv_b, norm_g):
    b, s, _ = ml_q.shape
    L = min(ML_CHUNK, s)
    tril = jnp.tril(jnp.ones((L, L), F32))
    fixed = lambda bi, ci: (0, 0)
    seq = lambda w: pl.BlockSpec((None, L, w), lambda bi, ci: (bi, ci, 0))
    return pl.pallas_call(
        _mlstm_kernel,
        grid=(b, s // L),
        in_specs=[pl.BlockSpec((L, L), fixed),
                  pl.BlockSpec((ML_CONV, 2 * ML_WIDTH), fixed),
                  pl.BlockSpec((1, 2 * ML_WIDTH), fixed),
                  pl.BlockSpec((1, ML_WIDTH), fixed),
                  seq(ML_WIDTH), seq(ML_WIDTH), seq(ML_WIDTH), seq(ML_WIDTH), seq(2 * LANES)],
        out_specs=seq(ML_WIDTH),
        out_shape=jax.ShapeDtypeStruct((b, s, ML_WIDTH), BF16),
        scratch_shapes=[pltpu.VMEM((L + 2 * SUBLANES, 2 * ML_WIDTH), F32),
                        pltpu.VMEM((ML_HEADS, ML_HEAD_DIM, 2 * ML_HEAD_DIM), F32),
                        pltpu.VMEM((SUBLANES, LANES), F32)],
        compiler_params=pltpu.CompilerParams(dimension_semantics=("arbitrary",) * 2, vmem_limit_bytes=VMEM_LIMIT),
        name="mlstm",
    )(tril, conv_w, conv_b, norm_g, ml_q, ml_k, ml_v, ml_o, gates)


def _extract_top_exact(s, n, out_ref, lead):
    rows = s.shape[0]
    iota = lax.broadcasted_iota(jnp.int32, s.shape, 0)
    for k in range(n):
        mx = jnp.max(s, axis=0, keepdims=True)
        out_ref[lead, k:k + 1, :] = mx
        first = jnp.min(jnp.where(s == mx, iota, rows), axis=0, keepdims=True)
        s = jnp.where(iota == first, -jnp.inf, s)
    return s


def _extract_top_fast(s, n, out_ref, lead, n_pad=0):
    for k in range(n):
        mx = jnp.max(s, axis=0, keepdims=True)
        out_ref[lead, k:k + 1, :] = mx
        s = jnp.where(s == mx, -jnp.inf, s)
    removed = jnp.sum(jnp.where(s == -jnp.inf, 1.0, 0.0), axis=0, keepdims=True)
    return s, removed - float(n + n_pad)


def _mid_kernel(a_ref, hm_ref, x_ref, wout_ref, g2_ref, wpq_ref, sk_ref,
                x1_ref, xnt_ref, need_ref, e1_ref, s2_ref, e2_ref, q_s, vals_s, cand_s):
    x1 = (x_ref[...]
          + jnp.dot(a_ref[...], wout_ref[0:DA_WIDTH, :], preferred_element_type=F32)
          + jnp.dot(hm_ref[...], wout_ref[DA_WIDTH:DA_WIDTH + ML_WIDTH, :], preferred_element_type=F32))
    x1_ref[...] = x1
    xn = x1 * lax.rsqrt(jnp.mean(x1 * x1, axis=-1, keepdims=True) + NORM_EPS) * g2_ref[...]
    xnt = xn.T.astype(BF16)
    tiles = [slice(i * PEER_UNIT_TOKENS, (i + 1) * PEER_UNIT_TOKENS) for i in range(xnt.shape[1] // PEER_UNIT_TOKENS)]
    for i, ts in enumerate(tiles):
        xnt_ref[i] = xnt[:, ts]
    q = jnp.dot(xn.astype(BF16), wpq_ref[...], preferred_element_type=F32).astype(BF16)
    for hc in range(2 * PEER_HEADS):
        q_s[hc] = q[:, hc * PEER_HALF:(hc + 1) * PEER_HALF]
    tt = x1.shape[0]
    cand_s[N_CAND:N_CAND_PAD, :] = jnp.full((N_CAND_PAD - N_CAND, tt), -jnp.inf, F32)

    def head(h, carry):
        s1 = _nt_dot(sk_ref[2 * h], q_s[2 * h])
        s2 = _nt_dot(sk_ref[2 * h + 1], q_s[2 * h + 1])

        def select(extract):
            r1 = extract(s1, 0)
            r2 = extract(s2, 1)
            for i, cnt in enumerate(CAND_COUNTS):
                off = CAND_OFFSETS[i]
                cand_s[off:off + cnt, :] = vals_s[0, i:i + 1, :] + vals_s[1, 0:cnt, :]
            r3 = extract(cand_s[...], 2, N_CAND_PAD - N_CAND)
            cand_s[...] = r3[0]
            return jnp.maximum(jnp.maximum(r1[1], r2[1]), r3[1])

        def fast(s, lead, n_pad=0):
            return _extract_top_fast(s, PEER_TOPK, vals_s, lead, n_pad)

        def exact(s, lead, n_pad=0):
            return _extract_top_exact(s, PEER_TOPK, vals_s, lead), jnp.zeros((1, tt), F32)

        excess = select(fast)

        @pl.when(jnp.max(excess) > 0.0)
        def _():
            select(exact)

        top = vals_s[2, 0:PEER_TOPK, :]
        t0 = top[0:1, :]
        lse = t0 + jnp.log(jnp.sum(jnp.exp(top - t0), axis=0, keepdims=True))
        need = jnp.full(s1.shape, jnp.inf, F32)
        for i in reversed(range(PEER_TOPK)):
            cnt, off = CAND_COUNTS[i], CAND_OFFSETS[i]
            cut = jnp.min(jnp.where(cand_s[off:off + cnt, :] == -jnp.inf, vals_s[1, 0:cnt, :], jnp.inf),
                          axis=0, keepdims=True)
            need = jnp.where(s1 >= vals_s[0, i:i + 1, :], cut, need)
        c1 = vals_s[0, 0:1, :]
        e1 = jnp.exp(s1 - c1)
        e2 = 0.5 * jnp.exp(s2 - (lse - c1))
        for i, ts in enumerate(tiles):
            need_ref[i, h] = need[:, ts]
            e1_ref[i, h] = e1[:, ts]
            s2_ref[i, h] = s2[:, ts]
            e2_ref[i, h] = e2[:, ts]
        return carry

    lax.fori_loop(0, PEER_HEADS, head, 0)


def _mid(a, hm, x2, w_out, norm2_g, w_pq, sub_keys):
    t = x2.shape[0]
    tt = min(MID_TOKENS, t)
    tile = PEER_UNIT_TOKENS
    n_tile = tt // tile
    row = lambda i: (i, 0)
    fixed = lambda i: (0, 0)
    return pl.pallas_call(
        _mid_kernel,
        grid=(t // tt,),
        in_specs=[pl.BlockSpec((tt, DA_WIDTH), row),
                  pl.BlockSpec((tt, ML_WIDTH), row),
                  pl.BlockSpec((tt, D_MODEL), row),
                  pl.BlockSpec((D_MODEL, D_MODEL), fixed),
                  pl.BlockSpec((1, D_MODEL), fixed),
                  pl.BlockSpec((D_MODEL, 2 * PEER_HEADS * PEER_HALF), fixed),
                  pl.BlockSpec((2 * PEER_HEADS, PEER_NKEYS, PEER_HALF), lambda i: (0, 0, 0))],
        out_specs=[pl.BlockSpec((tt, D_MODEL), row),
                   pl.BlockSpec((n_tile, D_MODEL, tile), lambda i: (i, 0, 0))]
                  + [pl.BlockSpec((n_tile, PEER_HEADS, PEER_NKEYS, tile), lambda i: (i, 0, 0, 0))] * 4,
        out_shape=[jax.ShapeDtypeStruct((t, D_MODEL), F32),
                   jax.ShapeDtypeStruct((t // tile, D_MODEL, tile), BF16)]
                  + [jax.ShapeDtypeStruct((t // tile, PEER_HEADS, PEER_NKEYS, tile), F32)] * 4,
        scratch_shapes=[pltpu.VMEM((2 * PEER_HEADS, tt, PEER_HALF), BF16),
                        pltpu.VMEM((3, 2 * SUBLANES, tt), F32),
                        pltpu.VMEM((N_CAND_PAD, tt), F32)],
        compiler_params=pltpu.CompilerParams(dimension_semantics=("arbitrary",), vmem_limit_bytes=VMEM_LIMIT),
        name="mid",
    )(a, hm, x2, w_out, norm2_g, w_pq, sub_keys)


def _gate_unit(key0, act_ref, need_ref, e1_ref, s2_ref, e2_ref, g_ref, row0, th):
    for q in range(PEER_PIECE // PEER_NKEYS):
        need_rows = [need_ref[th, h, pl.ds(key0 + q, 1), :] for h in range(PEER_HEADS)]
        e1_rows = [e1_ref[th, h, pl.ds(key0 + q, 1), :] for h in range(PEER_HEADS)]
        for lt in range(PEER_UNIT_TOKENS // LANES):
            la = slice(lt * LANES, (lt + 1) * LANES)
            ls = slice(th * PEER_UNIT_TOKENS + lt * LANES, th * PEER_UNIT_TOKENS + (lt + 1) * LANES)
            w = None
            for h in range(PEER_HEADS):
                wh = jnp.where(s2_ref[th, h, :, la] >= need_rows[h][:, la], e2_ref[th, h, :, la], 0.0) * e1_rows[h][:, la]
                w = wh if w is None else w + wh
            a = act_ref[q * PEER_NKEYS:(q + 1) * PEER_NKEYS, la]
            g_ref[pl.ds(pl.multiple_of(row0 + q * PEER_NKEYS, PEER_NKEYS), PEER_NKEYS), ls] = (
                a * (1.0 + lax.erf(a * np.float32(np.sqrt(0.5)))) * w).astype(BF16)


def _peer_kernel(xnt_ref, u_ref, u_next_ref, vt_ref, need_ref, e1_ref, s2_ref, e2_ref, x1_ref, fg_ref, y_ref,
                 ga_s, gb_s, vkeep_s, acc_s, act_s):
    k = pl.program_id(1)
    per_chunk = PEER_ECHUNK // PEER_PIECE
    n_piece = 2 * per_chunk
    keys_per_piece = PEER_PIECE // PEER_NKEYS

    def act_unit(u_rows, th):
        act_s[th] = jnp.dot(u_rows, xnt_ref[th], preferred_element_type=F32)

    def u_piece(p):
        return u_ref[pl.ds(pl.multiple_of(p * PEER_PIECE, PEER_PIECE), PEER_PIECE), :]

    @pl.when(k == 0)
    def _():
        acc_s[...] = jnp.zeros(acc_s.shape, F32)
        gb_s[...] = jnp.zeros(gb_s.shape, BF16)
        vkeep_s[...] = jnp.zeros(vkeep_s.shape, BF16)
        act_unit(u_piece(0), 0)

    def piece(r, c, last):
        g_prev, g_cur = (gb_s, ga_s) if c == 0 else (ga_s, gb_s)
        p = c * per_chunk + r
        rows = pl.ds(pl.multiple_of(r * PEER_PIECE, PEER_PIECE), PEER_PIECE)
        v_rows = vkeep_s[rows, :] if c == 0 else vt_ref[rows, 0:PEER_ECHUNK]
        for th in range(2):
            ts = slice(th * PEER_UNIT_TOKENS, (th + 1) * PEER_UNIT_TOKENS)
            if th == 0:
                act_unit(u_piece(p), 1)
            else:
                act_unit(u_next_ref[...] if last else u_piece(p + 1), 0)
            acc_s[rows, ts] += jnp.dot(v_rows, g_prev[:, ts], preferred_element_type=F32)
            _gate_unit(p * keys_per_piece, act_s.at[th], need_ref, e1_ref, s2_ref, e2_ref, g_cur,
                       r * PEER_PIECE, th)

    def rolled(c, first, n, last):
        def body(r, carry):
            piece(r, c, last)
            return carry
        lax.fori_loop(first, jnp.minimum(k + first + n, first + n), body, 0)

    rolled(0, 0, per_chunk, False)
    rolled(1, 0, per_chunk - 1, False)
    rolled(1, per_chunk - 1, 1, True)
    vkeep_s[...] = vt_ref[:, PEER_ECHUNK:2 * PEER_ECHUNK]

    @pl.when(k == pl.num_programs(1) - 1)
    def _():
        acc = acc_s[...] + jnp.dot(vkeep_s[...], gb_s[...], preferred_element_type=F32)
        x2 = x1_ref[...] + acc.T
        y_ref[...] = x2 * lax.rsqrt(jnp.mean(x2 * x2, axis=-1, keepdims=True) + NORM_EPS) * fg_ref[...]


def _peer(xnt, u_bf, vt_bf, need, e1, s2, e2, x1, final_g):
    t = x1.shape[0]
    tt = min(PEER_TOKENS, t)
    assert tt == 2 * PEER_UNIT_TOKENS and need.shape[-1] == PEER_UNIT_TOKENS
    step = 2 * PEER_ECHUNK
    n_step = PEER_EXPERTS // step
    step_keys = step // PEER_NKEYS
    halves = tt // PEER_UNIT_TOKENS
    key_rows = pl.BlockSpec((halves, PEER_HEADS, step_keys, PEER_UNIT_TOKENS), lambda i, e: (i, 0, e, 0))
    all_keys = pl.BlockSpec((halves, PEER_HEADS, PEER_NKEYS, PEER_UNIT_TOKENS), lambda i, e: (i, 0, 0, 0))
    return pl.pallas_call(
        _peer_kernel,
        grid=(t // tt, n_step),
        in_specs=[pl.BlockSpec((halves, D_MODEL, PEER_UNIT_TOKENS), lambda i, e: (i, 0, 0)),
                  pl.BlockSpec((step, D_MODEL), lambda i, e: (e, 0)),
                  pl.BlockSpec((PEER_PIECE, D_MODEL),
                               lambda i, e: (jnp.minimum(e + 1, n_step - 1) * (step // PEER_PIECE), 0)),
                  pl.BlockSpec((D_MODEL, step), lambda i, e: (0, e)),
                  key_rows, key_rows, all_keys, all_keys,
                  pl.BlockSpec((tt, D_MODEL), lambda i, e: (i, 0)),
                  pl.BlockSpec((1, D_MODEL), lambda i, e: (0, 0))],
        out_specs=pl.BlockSpec((tt, D_MODEL), lambda i, e: (i, 0)),
        out_shape=jax.ShapeDtypeStruct((t, D_MODEL), F32),
        scratch_shapes=[pltpu.VMEM((PEER_ECHUNK, tt), BF16),
                        pltpu.VMEM((PEER_ECHUNK, tt), BF16),
                        pltpu.VMEM((D_MODEL, PEER_ECHUNK), BF16),
                        pltpu.VMEM((D_MODEL, tt), F32),
                        pltpu.VMEM((2, PEER_PIECE, PEER_UNIT_TOKENS), F32)],
        compiler_params=pltpu.CompilerParams(dimension_semantics=("arbitrary",) * 2, vmem_limit_bytes=VMEM_LIMIT),
        name="peer_experts",
    )(xnt, u_bf, u_bf, vt_bf, need, e1, s2, e2, x1, final_g)


def _alibi_slopes(n):
    start = 2.0 ** (-8.0 / n)
    return jnp.asarray(np.array([start ** (h + 1) for h in range(n)], dtype=np.float32))


def kernel(x, norm1_g, w_in, conv_w, conv_b, b_igate, b_fgate, lam_q1, lam_k1, lam_q2, lam_k2, da_subln_g, ml_norm_g, w_out, norm2_g, w_pq, sub_keys, u_emb, v_emb, final_g):
    bsz, seq, d = x.shape
    t = bsz * seq
    x2 = x.reshape(t, d)
    l = 0
    w_main = w_in[l][:, :N_MAIN].astype(BF16)
    gate_pad = jnp.zeros((d, LANES - ML_HEADS), F32)
    w_gate = jnp.concatenate([w_in[l][:, N_MAIN:N_MAIN + ML_HEADS], gate_pad,
                              w_in[l][:, N_MAIN + ML_HEADS:], gate_pad], axis=1).astype(BF16)
    bias_pad = jnp.zeros((LANES - ML_HEADS,), F32)
    b_gate = jnp.concatenate([b_igate[l], bias_pad, b_fgate[l], bias_pad])[None, :]
    lamv = jnp.stack([lam_q1[l], lam_k1[l], lam_q2[l], lam_k2[l]])

    q0, q1, da_k, da_v, ml_q, ml_k, ml_v, ml_o, gates = _inproj(x2, norm1_g[l][None, :], w_main, w_gate, b_gate)
    seq3 = lambda z: z.reshape(bsz, seq, z.shape[-1])
    a = _diff_attention(seq3(q0), seq3(q1), seq3(da_k), seq3(da_v), _alibi_slopes(DA_HEADS), lamv,
                        da_subln_g[l][None, :])
    hm = _mlstm(seq3(ml_q), seq3(ml_k), seq3(ml_v), seq3(ml_o), seq3(gates),
                conv_w[l], conv_b[l][None, :], ml_norm_g[l][None, :])
    x1, xnt, need, e1, s2, e2 = _mid(a.reshape(t, DA_WIDTH), hm.reshape(t, ML_WIDTH), x2, w_out[l].astype(BF16),
                                     norm2_g[l][None, :], w_pq[l].astype(BF16),
                                     sub_keys[l].reshape(2 * PEER_HEADS, PEER_NKEYS, PEER_HALF).astype(BF16))
    y = _peer(xnt, u_emb[l].astype(BF16), v_emb[l].T.astype(BF16), need, e1, s2, e2, x1, final_g[None, :])
    return y.reshape(bsz, seq, d)
```

```python
import numpy as np
import jax
import jax.numpy as jnp
from jax import lax
from jax.experimental import pallas as pl
from jax.experimental.pallas import tpu as pltpu

F32 = jnp.float32
BF16 = jnp.bfloat16

D_MODEL = 1024
DA_HEADS = 4
DA_QK_DIM = 64
DA_V_DIM = 128
DA_WIDTH = 512
ML_HEADS = 4
ML_HEAD_DIM = 128
ML_WIDTH = 512
ML_CONV = 4
N_MAIN = 7 * 512
PEER_HEADS = 8
PEER_NKEYS = 128
PEER_EXPERTS = PEER_NKEYS * PEER_NKEYS
PEER_HALF = 128
PEER_TOPK = 16
NORM_EPS = 1e-6
SUBLN_EPS = 1e-5
NEG_INF = -1e30
LAM_INIT = 0.2
LANES = 128
SUBLANES = 8

IN_TOKENS = 512
DA_TQ = 256
DA_TK = 512
DA_HEADS_PER_STEP = 2
ML_CHUNK = 256
MID_TOKENS = 512
PEER_TOKENS = 512
PEER_ECHUNK = 1024
PEER_PIECE = 512
PEER_UNIT_TOKENS = 256
CAND_COUNTS = tuple(PEER_TOPK // (i + 1) for i in range(PEER_TOPK))
CAND_OFFSETS = tuple(sum(CAND_COUNTS[:i]) for i in range(PEER_TOPK))
N_CAND = sum(CAND_COUNTS)
N_CAND_PAD = -(-N_CAND // SUBLANES) * SUBLANES
VMEM_LIMIT = 56 * 1024 * 1024


def _nt_dot(a, b):
    return lax.dot_general(a, b, (((1,), (1,)), ((), ())), preferred_element_type=F32)


def _inproj_kernel(x_ref, g_ref, w_ref, wg_ref, bg_ref,
                   q0_ref, q1_ref, dak_ref, dav_ref, mlq_ref, mlk_ref, mlv_ref, mlo_ref, gate_ref):
    x = x_ref[...]
    h = x * lax.rsqrt(jnp.mean(x * x, axis=-1, keepdims=True) + NORM_EPS) * g_ref[...]
    hb = h.astype(BF16)

    def mm(group):
        return jnp.dot(hb, w_ref[:, group * 512:(group + 1) * 512], preferred_element_type=F32)

    q = mm(0) * (DA_QK_DIM ** -0.5)
    lane = lax.broadcasted_iota(jnp.int32, q.shape, 1)
    first = (lane % (2 * DA_QK_DIM)) < DA_QK_DIM
    q0_ref[...] = jnp.where(first, q, 0.0).astype(BF16)
    q1_ref[...] = jnp.where(first, 0.0, q).astype(BF16)
    dak_ref[...] = mm(1).astype(BF16)
    dav_ref[...] = mm(2).astype(BF16)
    mlq_ref[...] = mm(3)
    mlk_ref[...] = mm(4)
    mlv_ref[...] = mm(5).astype(BF16)
    mlo_ref[...] = mm(6)
    gate_ref[...] = jnp.dot(hb, wg_ref[...], preferred_element_type=F32) + bg_ref[...]


def _inproj(x2, norm1_g, w_main, w_gate, b_gate):
    t = x2.shape[0]
    tt = min(IN_TOKENS, t)
    row = lambda i: (i, 0)
    fixed = lambda i: (0, 0)
    wide = lambda dt: jax.ShapeDtypeStruct((t, 512), dt)
    return pl.pallas_call(
        _inproj_kernel,
        grid=(t // tt,),
        in_specs=[pl.BlockSpec((tt, D_MODEL), row),
                  pl.BlockSpec((1, D_MODEL), fixed),
                  pl.BlockSpec((D_MODEL, N_MAIN), fixed),
                  pl.BlockSpec((D_MODEL, 2 * LANES), fixed),
                  pl.BlockSpec((1, 2 * LANES), fixed)],
        out_specs=[pl.BlockSpec((tt, 512), row)] * 8 + [pl.BlockSpec((tt, 2 * LANES), row)],
        out_shape=[wide(BF16), wide(BF16), wide(BF16), wide(BF16), wide(F32), wide(F32), wide(BF16), wide(F32),
                   jax.ShapeDtypeStruct((t, 2 * LANES), F32)],
        compiler_params=pltpu.CompilerParams(dimension_semantics=("arbitrary",), vmem_limit_bytes=VMEM_LIMIT),
        name="inproj",
    )(x2, norm1_g, w_main, w_gate, b_gate)


def _da_kernel(slope_ref, lamv_ref, g_ref, q0_ref, q1_ref, k_ref, v_ref, o_ref, m_s, acc_s, s_s):
    hp = pl.program_id(1)
    i = pl.program_id(2)
    tq = q0_ref.shape[0]
    m_s[...] = jnp.full(m_s.shape, NEG_INF, F32)
    acc_s[...] = jnp.zeros(acc_s.shape, F32)
    one_col = jnp.where(lax.broadcasted_iota(jnp.int32, (DA_TK, LANES), 1) == 0, 1.0, 0.0).astype(BF16)

    def scores(j, slot):
        kstart = pl.multiple_of(j * DA_TK, DA_TK)
        krel = (kstart - i * tq + lax.broadcasted_iota(jnp.int32, (1, DA_TK), 1)).astype(F32)
        for hh in range(DA_HEADS_PER_STEP):
            hs = slice(hh * LANES, (hh + 1) * LANES)
            q2 = jnp.concatenate([q0_ref[:, hs], q1_ref[:, hs]], axis=0)
            s_s[slot, hh] = (_nt_dot(q2, k_ref[pl.ds(kstart, DA_TK), hs])
                             + slope_ref[hp * DA_HEADS_PER_STEP + hh] * krel)

    def softmax_pv(j, slot, masked):
        kstart = pl.multiple_of(j * DA_TK, DA_TK)
        for hh in range(DA_HEADS_PER_STEP):
            hs = slice(hh * LANES, (hh + 1) * LANES)
            v_ext = jnp.concatenate([v_ref[pl.ds(kstart, DA_TK), hs], one_col], axis=1)
            s = s_s[slot, hh]
            if masked:
                row = lax.broadcasted_iota(jnp.int32, s.shape, 0)
                col = lax.broadcasted_iota(jnp.int32, s.shape, 1)
                rq = jnp.where(row >= tq, row - tq, row)
                s = jnp.where(col + (kstart - i * tq) <= rq, s, NEG_INF)
            m_prev = m_s[hh]
            m_new = jnp.maximum(m_prev, jnp.max(s, axis=-1, keepdims=True))
            alpha = jnp.exp(m_prev - m_new)
            p = jnp.exp(s - m_new).astype(BF16)
            acc_s[hh] = alpha * acc_s[hh] + jnp.dot(p, v_ext, preferred_element_type=F32)
            m_s[hh] = m_new

    last = (i * tq) // DA_TK

    def block(j, slot):
        @pl.when(j < last)
        def _():
            scores(j + 1, 1 - slot)
            softmax_pv(j, slot, False)

        @pl.when(j == last)
        def _():
            softmax_pv(j, slot, True)

    def pair(jj, carry):
        block(2 * jj, 0)
        block(2 * jj + 1, 1)
        return carry

    scores(0, 0)
    lax.fori_loop(0, last // 2 + 1, pair, 0)

    lamv = lamv_ref[...]
    lam = (jnp.exp(jnp.sum(lamv[0:1] * lamv[1:2], axis=-1, keepdims=True))
           - jnp.exp(jnp.sum(lamv[2:3] * lamv[3:4], axis=-1, keepdims=True)) + LAM_INIT)
    for hh in range(DA_HEADS_PER_STEP):
        acc = acc_s[hh]
        o = acc[:, 0:DA_V_DIM] / acc[:, DA_V_DIM:DA_V_DIM + 1]
        a = o[:tq] - lam * o[tq:]
        y = a * lax.rsqrt(jnp.mean(a * a, axis=-1, keepdims=True) + SUBLN_EPS) * g_ref[...]
        o_ref[:, hh * LANES:(hh + 1) * LANES] = (y * (1.0 - LAM_INIT)).astype(BF16)


def _diff_attention(q0, q1, k, v, slopes, lamv, subln_g):
    b, s, _ = q0.shape
    assert DA_TK % DA_TQ == 0 and s % DA_TK == 0
    hw = DA_HEADS_PER_STEP * LANES
    qspec = pl.BlockSpec((None, DA_TQ, hw), lambda bi, hi, qi: (bi, qi, hi))
    kvspec = pl.BlockSpec((None, s, hw), lambda bi, hi, qi: (bi, 0, hi))
    return pl.pallas_call(
        _da_kernel,
        grid=(b, DA_HEADS // DA_HEADS_PER_STEP, s // DA_TQ),
        in_specs=[pl.BlockSpec(memory_space=pltpu.SMEM),
                  pl.BlockSpec((4, DA_QK_DIM), lambda bi, hi, qi: (0, 0)),
                  pl.BlockSpec((1, DA_V_DIM), lambda bi, hi, qi: (0, 0)),
                  qspec, qspec, kvspec, kvspec],
        out_specs=qspec,
        out_shape=jax.ShapeDtypeStruct((b, s, DA_WIDTH), BF16),
        scratch_shapes=[pltpu.VMEM((DA_HEADS_PER_STEP, 2 * DA_TQ, 1), F32),
                        pltpu.VMEM((DA_HEADS_PER_STEP, 2 * DA_TQ, 2 * DA_V_DIM), F32),
                        pltpu.VMEM((2, DA_HEADS_PER_STEP, 2 * DA_TQ, DA_TK), F32)],
        compiler_params=pltpu.CompilerParams(dimension_semantics=("arbitrary",) * 3, vmem_limit_bytes=VMEM_LIMIT),
        name="diff_attention",
    )(slopes, lamv, subln_g, q0, q1, k, v)


def _mlstm_kernel(tril_ref, cw_ref, cb_ref, ng_ref, q_ref, k_ref, v_ref, og_ref, gate_ref, out_ref,
                  xbuf, cn_s, m_s):
    c = pl.program_id(1)
    L = q_ref.shape[0]

    @pl.when(c == 0)
    def _():
        xbuf[0:SUBLANES, :] = jnp.zeros((SUBLANES, 2 * ML_WIDTH), F32)
        cn_s[...] = jnp.zeros(cn_s.shape, F32)
        m_s[...] = jnp.full(m_s.shape, NEG_INF, F32)

    @pl.when(c > 0)
    def _():
        xbuf[0:SUBLANES, :] = xbuf[L:L + SUBLANES, :]

    xbuf[SUBLANES:SUBLANES + L, 0:ML_WIDTH] = q_ref[...]
    xbuf[SUBLANES:SUBLANES + L, ML_WIDTH:2 * ML_WIDTH] = k_ref[...]
    y = cb_ref[...] + cw_ref[0:1, :] * xbuf[SUBLANES - 3:SUBLANES - 3 + L, :]
    for j in range(1, ML_CONV):
        y = y + cw_ref[j:j + 1, :] * xbuf[SUBLANES - 3 + j:SUBLANES - 3 + j + L, :]
    qk = y * jax.nn.sigmoid(y)

    gates = gate_ref[...]
    i_log = gates[:, 0:LANES]
    f_log = jax.nn.log_sigmoid(gates[:, LANES:2 * LANES])
    b_all = jnp.dot(tril_ref[...], f_log, preferred_element_type=F32,
                    precision=lax.Precision.HIGHEST)
    r_all = i_log - b_all
    r_all_t = r_all.T
    row = lax.broadcasted_iota(jnp.int32, (L, L), 0)
    col = lax.broadcasted_iota(jnp.int32, (L, L), 1)
    tri = col <= row
    one_col = jnp.where(lax.broadcasted_iota(jnp.int32, (L, LANES), 1) == 0, 1.0, 0.0).astype(BF16)

    for h in range(ML_HEADS):
        hs = slice(h * ML_HEAD_DIM, (h + 1) * ML_HEAD_DIM)
        b_col = b_all[:, h:h + 1]
        r_col = r_all[:, h:h + 1]
        r_row = r_all_t[h:h + 1, :]
        m_prev = m_s[h:h + 1, 0:1]
        d = jnp.where(tri, b_col + r_row, NEG_INF)
        inter = b_col + m_prev
        mt = jnp.maximum(inter, jnp.max(d, axis=-1, keepdims=True))
        w_intra = jnp.exp(d - mt)
        w_inter = jnp.exp(inter - mt)
        qb = qk[:, hs].astype(BF16)
        kf = qk[:, ML_WIDTH + h * ML_HEAD_DIM:ML_WIDTH + (h + 1) * ML_HEAD_DIM] * (ML_HEAD_DIM ** -0.5)
        s = _nt_dot(qb, kf.astype(BF16)) * w_intra
        v_ext = jnp.concatenate([v_ref[:, hs], one_col], axis=1)
        cn = cn_s[h]
        num_ext = (w_inter * jnp.dot(qb, cn.astype(BF16), preferred_element_type=F32)
                   + jnp.dot(s.astype(BF16), v_ext, preferred_element_type=F32))
        num = num_ext[:, 0:ML_HEAD_DIM]
        den = num_ext[:, ML_HEAD_DIM:ML_HEAD_DIM + 1]
        hh = num / jnp.maximum(jnp.abs(den), jnp.exp(-mt))
        m_new = mt[L - 1:L, :]
        b_last = b_col[L - 1:L, :]
        ws = jnp.exp(b_last + r_col - m_new)
        decay = jnp.exp(b_last + m_prev - m_new)
        kw_t = (kf * ws).T.astype(BF16)
        cn_s[h] = decay * cn + jnp.dot(kw_t, v_ext, preferred_element_type=F32)
        m_s[h:h + 1, :] = jnp.broadcast_to(m_new, (1, LANES))
        hn = hh * lax.rsqrt(jnp.mean(hh * hh, axis=-1, keepdims=True) + NORM_EPS) * ng_ref[:, hs]
        out_ref[:, hs] = (hn * jax.nn.sigmoid(og_ref[:, hs])).astype(BF16)


def _mlstm(ml_q, ml_k, ml_v, ml_o, gates, conv_w, conv_b, norm_g):
    b, s, _ = ml_q.shape
    L = min(ML_CHUNK, s)
    tril = jnp.tril(jnp.ones((L, L), F32))
    fixed = lambda bi, ci: (0, 0)
    seq = lambda w: pl.BlockSpec((None, L, w), lambda bi, ci: (bi, ci, 0))
    return pl.pallas_call(
        _mlstm_kernel,
        grid=(b, s // L),
        in_specs=[pl.BlockSpec((L, L), fixed),
                  pl.BlockSpec((ML_CONV, 2 * ML_WIDTH), fixed),
                  pl.BlockSpec((1, 2 * ML_WIDTH), fixed),
                  pl.BlockSpec((1, ML_WIDTH), fixed),
                  seq(ML_WIDTH), seq(ML_WIDTH), seq(ML_WIDTH), seq(ML_WIDTH), seq(2 * LANES)],
        out_specs=seq(ML_WIDTH),
        out_shape=jax.ShapeDtypeStruct((b, s, ML_WIDTH), BF16),
        scratch_shapes=[pltpu.VMEM((L + 2 * SUBLANES, 2 * ML_WIDTH), F32),
                        pltpu.VMEM((ML_HEADS, ML_HEAD_DIM, 2 * ML_HEAD_DIM), F32),
                        pltpu.VMEM((SUBLANES, LANES), F32)],
        compiler_params=pltpu.CompilerParams(dimension_semantics=("arbitrary",) * 2, vmem_limit_bytes=VMEM_LIMIT),
        name="mlstm",
    )(tril, conv_w, conv_b, norm_g, ml_q, ml_k, ml_v, ml_o, gates)


def _extract_top_exact(s, n, out_ref, lead):
    rows = s.shape[0]
    iota = lax.broadcasted_iota(jnp.int32, s.shape, 0)
    for k in range(n):
        mx = jnp.max(s, axis=0, keepdims=True)
        out_ref[lead, k:k + 1, :] = mx
        first = jnp.min(jnp.where(s == mx, iota, rows), axis=0, keepdims=True)
        s = jnp.where(iota == first, -jnp.inf, s)
    return s


def _extract_top_fast(s, n, out_ref, lead, n_pad=0):
    for k in range(n):
        mx = jnp.max(s, axis=0, keepdims=True)
        out_ref[lead, k:k + 1, :] = mx
        s = jnp.where(s == mx, -jnp.inf, s)
    removed = jnp.sum(jnp.where(s == -jnp.inf, 1.0, 0.0), axis=0, keepdims=True)
    return s, removed - float(n + n_pad)


def _mid_kernel(a_ref, hm_ref, x_ref, wout_ref, g2_ref, wpq_ref, sk_ref,
                x1_ref, xnt_ref, need_ref, e1_ref, s2_ref, e2_ref, q_s, vals_s, cand_s):
    x1 = (x_ref[...]
          + jnp.dot(a_ref[...], wout_ref[0:DA_WIDTH, :], preferred_element_type=F32)
          + jnp.dot(hm_ref[...], wout_ref[DA_WIDTH:DA_WIDTH + ML_WIDTH, :], preferred_element_type=F32))
    x1_ref[...] = x1
    xn = x1 * lax.rsqrt(jnp.mean(x1 * x1, axis=-1, keepdims=True) + NORM_EPS) * g2_ref[...]
    xnt = xn.T.astype(BF16)
    tiles = [slice(i * PEER_UNIT_TOKENS, (i + 1) * PEER_UNIT_TOKENS) for i in range(xnt.shape[1] // PEER_UNIT_TOKENS)]
    for i, ts in enumerate(tiles):
        xnt_ref[i] = xnt[:, ts]
    q = jnp.dot(xn.astype(BF16), wpq_ref[...], preferred_element_type=F32).astype(BF16)
    for hc in range(2 * PEER_HEADS):
        q_s[hc] = q[:, hc * PEER_HALF:(hc + 1) * PEER_HALF]
    tt = x1.shape[0]
    cand_s[N_CAND:N_CAND_PAD, :] = jnp.full((N_CAND_PAD - N_CAND, tt), -jnp.inf, F32)

    def head(h, carry):
        s1 = _nt_dot(sk_ref[2 * h], q_s[2 * h])
        s2 = _nt_dot(sk_ref[2 * h + 1], q_s[2 * h + 1])

        def select(extract):
            r1 = extract(s1, 0)
            r2 = extract(s2, 1)
            for i, cnt in enumerate(CAND_COUNTS):
                off = CAND_OFFSETS[i]
                cand_s[off:off + cnt, :] = vals_s[0, i:i + 1, :] + vals_s[1, 0:cnt, :]
            r3 = extract(cand_s[...], 2, N_CAND_PAD - N_CAND)
            cand_s[...] = r3[0]
            return jnp.maximum(jnp.maximum(r1[1], r2[1]), r3[1])

        def fast(s, lead, n_pad=0):
            return _extract_top_fast(s, PEER_TOPK, vals_s, lead, n_pad)

        def exact(s, lead, n_pad=0):
            return _extract_top_exact(s, PEER_TOPK, vals_s, lead), jnp.zeros((1, tt), F32)

        excess = select(fast)

        @pl.when(jnp.max(excess) > 0.0)
        def _():
            select(exact)

        top = vals_s[2, 0:PEER_TOPK, :]
        t0 = top[0:1, :]
        lse = t0 + jnp.log(jnp.sum(jnp.exp(top - t0), axis=0, keepdims=True))
        need = jnp.full(s1.shape, jnp.inf, F32)
        for i in reversed(range(PEER_TOPK)):
            cnt, off = CAND_COUNTS[i], CAND_OFFSETS[i]
            cut = jnp.min(jnp.where(cand_s[off:off + cnt, :] == -jnp.inf, vals_s[1, 0:cnt, :], jnp.inf),
                          axis=0, keepdims=True)
            need = jnp.where(s1 >= vals_s[0, i:i + 1, :], cut, need)
        c1 = vals_s[0, 0:1, :]
        e1 = jnp.exp(s1 - c1)
        e2 = 0.5 * jnp.exp(s2 - (lse - c1))
        for i, ts in enumerate(tiles):
            need_ref[i, h] = need[:, ts]
            e1_ref[i, h] = e1[:, ts]
            s2_ref[i, h] = s2[:, ts]
            e2_ref[i, h] = e2[:, ts]
        return carry

    lax.fori_loop(0, PEER_HEADS, head, 0)


def _mid(a, hm, x2, w_out, norm2_g, w_pq, sub_keys):
    t = x2.shape[0]
    tt = min(MID_TOKENS, t)
    tile = PEER_UNIT_TOKENS
    n_tile = tt // tile
    row = lambda i: (i, 0)
    fixed = lambda i: (0, 0)
    return pl.pallas_call(
        _mid_kernel,
        grid=(t // tt,),
        in_specs=[pl.BlockSpec((tt, DA_WIDTH), row),
                  pl.BlockSpec((tt, ML_WIDTH), row),
                  pl.BlockSpec((tt, D_MODEL), row),
                  pl.BlockSpec((D_MODEL, D_MODEL), fixed),
                  pl.BlockSpec((1, D_MODEL), fixed),
                  pl.BlockSpec((D_MODEL, 2 * PEER_HEADS * PEER_HALF), fixed),
                  pl.BlockSpec((2 * PEER_HEADS, PEER_NKEYS, PEER_HALF), lambda i: (0, 0, 0))],
        out_specs=[pl.BlockSpec((tt, D_MODEL), row),
                   pl.BlockSpec((n_tile, D_MODEL, tile), lambda i: (i, 0, 0))]
                  + [pl.BlockSpec((n_tile, PEER_HEADS, PEER_NKEYS, tile), lambda i: (i, 0, 0, 0))] * 4,
        out_shape=[jax.ShapeDtypeStruct((t, D_MODEL), F32),
                   jax.ShapeDtypeStruct((t // tile, D_MODEL, tile), BF16)]
                  + [jax.ShapeDtypeStruct((t // tile, PEER_HEADS, PEER_NKEYS, tile), F32)] * 4,
        scratch_shapes=[pltpu.VMEM((2 * PEER_HEADS, tt, PEER_HALF), BF16),
                        pltpu.VMEM((3, 2 * SUBLANES, tt), F32),
                        pltpu.VMEM((N_CAND_PAD, tt), F32)],
        compiler_params=pltpu.CompilerParams(dimension_semantics=("arbitrary",), vmem_limit_bytes=VMEM_LIMIT),
        name="mid",
    )(a, hm, x2, w_out, norm2_g, w_pq, sub_keys)


def _gate_unit(key0, act_ref, need_ref, e1_ref, s2_ref, e2_ref, g_ref, row0, th):
    for q in range(PEER_PIECE // PEER_NKEYS):
        need_rows = [need_ref[th, h, pl.ds(key0 + q, 1), :] for h in range(PEER_HEADS)]
        e1_rows = [e1_ref[th, h, pl.ds(key0 + q, 1), :] for h in range(PEER_HEADS)]
        for lt in range(PEER_UNIT_TOKENS // LANES):
            la = slice(lt * LANES, (lt + 1) * LANES)
            ls = slice(th * PEER_UNIT_TOKENS + lt * LANES, th * PEER_UNIT_TOKENS + (lt + 1) * LANES)
            w = None
            for h in range(PEER_HEADS):
                wh = jnp.where(s2_ref[th, h, :, la] >= need_rows[h][:, la], e2_ref[th, h, :, la], 0.0) * e1_rows[h][:, la]
                w = wh if w is None else w + wh
            a = act_ref[q * PEER_NKEYS:(q + 1) * PEER_NKEYS, la]
            g_ref[pl.ds(pl.multiple_of(row0 + q * PEER_NKEYS, PEER_NKEYS), PEER_NKEYS), ls] = (
                a * (1.0 + lax.erf(a * np.float32(np.sqrt(0.5)))) * w).astype(BF16)


def _peer_kernel(xnt_ref, u_ref, u_next_ref, vt_a_ref, vt_bprev_ref, vt_last_ref, need_ref, e1_ref, s2_ref, e2_ref,
                 x1_ref, fg_ref, y_ref, ga_s, gb_s, acc_s, act_s):
    k = pl.program_id(1)
    per_chunk = PEER_ECHUNK // PEER_PIECE
    n_piece = 2 * per_chunk
    keys_per_piece = PEER_PIECE // PEER_NKEYS

    def act_unit(u_rows, th):
        act_s[th] = jnp.dot(u_rows, xnt_ref[th], preferred_element_type=F32)

    def u_piece(p):
        return u_ref[pl.ds(pl.multiple_of(p * PEER_PIECE, PEER_PIECE), PEER_PIECE), :]

    @pl.when(k == 0)
    def _():
        acc_s[...] = jnp.zeros(acc_s.shape, F32)
        gb_s[...] = jnp.zeros(gb_s.shape, BF16)
        act_unit(u_piece(0), 0)

    def piece(r, c, last):
        g_prev, g_cur = (gb_s, ga_s) if c == 0 else (ga_s, gb_s)
        p = c * per_chunk + r
        rows = pl.ds(pl.multiple_of(r * PEER_PIECE, PEER_PIECE), PEER_PIECE)
        v_rows = vt_bprev_ref[rows, :] if c == 0 else vt_a_ref[rows, :]
        for th in range(2):
            ts = slice(th * PEER_UNIT_TOKENS, (th + 1) * PEER_UNIT_TOKENS)
            if th == 0:
                act_unit(u_piece(p), 1)
            else:
                act_unit(u_next_ref[...] if last else u_piece(p + 1), 0)
            acc_s[rows, ts] += jnp.dot(v_rows, g_prev[:, ts], preferred_element_type=F32)
            _gate_unit(p * keys_per_piece, act_s.at[th], need_ref, e1_ref, s2_ref, e2_ref, g_cur,
                       r * PEER_PIECE, th)

    def rolled(c, first, n, last):
        def body(r, carry):
            piece(r, c, last)
            return carry
        lax.fori_loop(first, jnp.minimum(k + first + n, first + n), body, 0)

    rolled(0, 0, per_chunk, False)
    rolled(1, 0, per_chunk - 1, False)
    rolled(1, per_chunk - 1, 1, True)

    @pl.when(k == pl.num_programs(1) - 1)
    def _():
        acc = acc_s[...] + jnp.dot(vt_last_ref[...], gb_s[...], preferred_element_type=F32)
        x2 = x1_ref[...] + acc.T
        y_ref[...] = x2 * lax.rsqrt(jnp.mean(x2 * x2, axis=-1, keepdims=True) + NORM_EPS) * fg_ref[...]


def _peer(xnt, u_bf, vt_bf, need, e1, s2, e2, x1, final_g):
    t = x1.shape[0]
    tt = min(PEER_TOKENS, t)
    assert tt == 2 * PEER_UNIT_TOKENS and need.shape[-1] == PEER_UNIT_TOKENS
    step = 2 * PEER_ECHUNK
    n_step = PEER_EXPERTS // step
    step_keys = step // PEER_NKEYS
    halves = tt // PEER_UNIT_TOKENS
    key_rows = pl.BlockSpec((halves, PEER_HEADS, step_keys, PEER_UNIT_TOKENS), lambda i, e: (i, 0, e, 0))
    all_keys = pl.BlockSpec((halves, PEER_HEADS, PEER_NKEYS, PEER_UNIT_TOKENS), lambda i, e: (i, 0, 0, 0))
    return pl.pallas_call(
        _peer_kernel,
        grid=(t // tt, n_step),
        in_specs=[pl.BlockSpec((halves, D_MODEL, PEER_UNIT_TOKENS), lambda i, e: (i, 0, 0)),
                  pl.BlockSpec((step, D_MODEL), lambda i, e: (e, 0)),
                  pl.BlockSpec((PEER_PIECE, D_MODEL),
                               lambda i, e: (jnp.minimum(e + 1, n_step - 1) * (step // PEER_PIECE), 0)),
                  pl.BlockSpec((D_MODEL, PEER_ECHUNK), lambda i, e: (0, 2 * e)),
                  pl.BlockSpec((D_MODEL, PEER_ECHUNK), lambda i, e: (0, jnp.maximum(2 * e - 1, 0))),
                  pl.BlockSpec((D_MODEL, PEER_ECHUNK), lambda i, e: (0, 2 * n_step - 1)),
                  key_rows, key_rows, all_keys, all_keys,
                  pl.BlockSpec((tt, D_MODEL), lambda i, e: (i, 0)),
                  pl.BlockSpec((1, D_MODEL), lambda i, e: (0, 0))],
        out_specs=pl.BlockSpec((tt, D_MODEL), lambda i, e: (i, 0)),
        out_shape=jax.ShapeDtypeStruct((t, D_MODEL), F32),
        scratch_shapes=[pltpu.VMEM((PEER_ECHUNK, tt), BF16),
                        pltpu.VMEM((PEER_ECHUNK, tt), BF16),
                        pltpu.VMEM((D_MODEL, tt), F32),
                        pltpu.VMEM((2, PEER_PIECE, PEER_UNIT_TOKENS), F32)],
        compiler_params=pltpu.CompilerParams(dimension_semantics=("arbitrary",) * 2, vmem_limit_bytes=VMEM_LIMIT),
        name="peer_experts",
    )(xnt, u_bf, u_bf, vt_bf, vt_bf, vt_bf, need, e1, s2, e2, x1, final_g)


def _alibi_slopes(n):
    start = 2.0 ** (-8.0 / n)
    return jnp.asarray(np.array([start ** (h + 1) for h in range(n)], dtype=np.float32))


def kernel(x, norm1_g, w_in, conv_w, conv_b, b_igate, b_fgate, lam_q1, lam_k1, lam_q2, lam_k2, da_subln_g, ml_norm_g, w_out, norm2_g, w_pq, sub_keys, u_emb, v_emb, final_g):
    bsz, seq, d = x.shape
    t = bsz * seq
    x2 = x.reshape(t, d)
    l = 0
    w_main = w_in[l][:, :N_MAIN].astype(BF16)
    gate_pad = jnp.zeros((d, LANES - ML_HEADS), F32)
    w_gate = jnp.concatenate([w_in[l][:, N_MAIN:N_MAIN + ML_HEADS], gate_pad,
                              w_in[l][:, N_MAIN + ML_HEADS:], gate_pad], axis=1).astype(BF16)
    bias_pad = jnp.zeros((LANES - ML_HEADS,), F32)
    b_gate = jnp.concatenate([b_igate[l], bias_pad, b_fgate[l], bias_pad])[None, :]
    lamv = jnp.stack([lam_q1[l], lam_k1[l], lam_q2[l], lam_k2[l]])

    q0, q1, da_k, da_v, ml_q, ml_k, ml_v, ml_o, gates = _inproj(x2, norm1_g[l][None, :], w_main, w_gate, b_gate)
    seq3 = lambda z: z.reshape(bsz, seq, z.shape[-1])
    a = _diff_attention(seq3(q0), seq3(q1), seq3(da_k), seq3(da_v), _alibi_slopes(DA_HEADS), lamv,
                        da_subln_g[l][None, :])
    hm = _mlstm(seq3(ml_q), seq3(ml_k), seq3(ml_v), seq3(ml_o), seq3(gates),
                conv_w[l], conv_b[l][None, :], ml_norm_g[l][None, :])
    x1, xnt, need, e1, s2, e2 = _mid(a.reshape(t, DA_WIDTH), hm.reshape(t, ML_WIDTH), x2, w_out[l].astype(BF16),
                                     norm2_g[l][None, :], w_pq[l].astype(BF16),
                                     sub_keys[l].reshape(2 * PEER_HEADS, PEER_NKEYS, PEER_HALF).astype(BF16))
    y = _peer(xnt, u_emb[l].astype(BF16), v_emb[l].T.astype(BF16), need, e1, s2, e2, x1, final_g[None, :])
    return y.reshape(bsz, seq, d)
```

```python
import numpy as np
import jax
import jax.numpy as jnp
from jax import lax
from jax.experimental import pallas as pl
from jax.experimental.pallas import tpu as pltpu

F32 = jnp.float32
BF16 = jnp.bfloat16

D_MODEL = 1024
DA_HEADS = 4
DA_QK_DIM = 64
DA_V_DIM = 128
DA_WIDTH = 512
ML_HEADS = 4
ML_HEAD_DIM = 128
ML_WIDTH = 512
ML_CONV = 4
N_MAIN = 7 * 512
PEER_HEADS = 8
PEER_NKEYS = 128
PEER_EXPERTS = PEER_NKEYS * PEER_NKEYS
PEER_HALF = 128
PEER_TOPK = 16
NORM_EPS = 1e-6
SUBLN_EPS = 1e-5
NEG_INF = -1e30
LAM_INIT = 0.2
LANES = 128
SUBLANES = 8

IN_TOKENS = 512
DA_TQ = 512
DA_TK = 512
DA_HEADS_PER_STEP = 2
ML_CHUNK = 256
MID_TOKENS = 512
PEER_TOKENS = 512
PEER_ECHUNK = 1024
PEER_PIECE = 512
PEER_UNIT_TOKENS = 256
CAND_COUNTS = tuple(PEER_TOPK // (i + 1) for i in range(PEER_TOPK))
CAND_OFFSETS = tuple(sum(CAND_COUNTS[:i]) for i in range(PEER_TOPK))
N_CAND = sum(CAND_COUNTS)
N_CAND_PAD = -(-N_CAND // SUBLANES) * SUBLANES
VMEM_LIMIT = 56 * 1024 * 1024


def _nt_dot(a, b):
    return lax.dot_general(a, b, (((1,), (1,)), ((), ())), preferred_element_type=F32)


def _inproj_kernel(x_ref, g_ref, w_ref, wg_ref, bg_ref,
                   q0_ref, q1_ref, dak_ref, dav_ref, mlq_ref, mlk_ref, mlv_ref, mlo_ref, gate_ref):
    x = x_ref[...]
    h = x * lax.rsqrt(jnp.mean(x * x, axis=-1, keepdims=True) + NORM_EPS) * g_ref[...]
    hb = h.astype(BF16)

    def mm(group):
        return jnp.dot(hb, w_ref[:, group * 512:(group + 1) * 512], preferred_element_type=F32)

    q = mm(0) * (DA_QK_DIM ** -0.5)
    lane = lax.broadcasted_iota(jnp.int32, q.shape, 1)
    first = (lane % (2 * DA_QK_DIM)) < DA_QK_DIM
    q0_ref[...] = jnp.where(first, q, 0.0).astype(BF16)
    q1_ref[...] = jnp.where(first, 0.0, q).astype(BF16)
    dak_ref[...] = mm(1).astype(BF16)
    dav_ref[...] = mm(2).astype(BF16)
    mlq_ref[...] = mm(3)
    mlk_ref[...] = mm(4)
    mlv_ref[...] = mm(5).astype(BF16)
    mlo_ref[...] = mm(6)
    gate_ref[...] = jnp.dot(hb, wg_ref[...], preferred_element_type=F32) + bg_ref[...]


def _inproj(x2, norm1_g, w_main, w_gate, b_gate):
    t = x2.shape[0]
    tt = min(IN_TOKENS, t)
    row = lambda i: (i, 0)
    fixed = lambda i: (0, 0)
    wide = lambda dt: jax.ShapeDtypeStruct((t, 512), dt)
    return pl.pallas_call(
        _inproj_kernel,
        grid=(t // tt,),
        in_specs=[pl.BlockSpec((tt, D_MODEL), row),
                  pl.BlockSpec((1, D_MODEL), fixed),
                  pl.BlockSpec((D_MODEL, N_MAIN), fixed),
                  pl.BlockSpec((D_MODEL, 2 * LANES), fixed),
                  pl.BlockSpec((1, 2 * LANES), fixed)],
        out_specs=[pl.BlockSpec((tt, 512), row)] * 8 + [pl.BlockSpec((tt, 2 * LANES), row)],
        out_shape=[wide(BF16), wide(BF16), wide(BF16), wide(BF16), wide(F32), wide(F32), wide(BF16), wide(F32),
                   jax.ShapeDtypeStruct((t, 2 * LANES), F32)],
        compiler_params=pltpu.CompilerParams(dimension_semantics=("arbitrary",), vmem_limit_bytes=VMEM_LIMIT),
        name="inproj",
    )(x2, norm1_g, w_main, w_gate, b_gate)


def _da_kernel(slope_ref, lamv_ref, g_ref, q0_ref, q1_ref, k_ref, v_ref, o_ref, m_s, acc_s, s_s):
    hp = pl.program_id(1)
    i = pl.program_id(2)
    tq = q0_ref.shape[0]
    m_s[...] = jnp.full(m_s.shape, NEG_INF, F32)
    acc_s[...] = jnp.zeros(acc_s.shape, F32)
    one_col = jnp.where(lax.broadcasted_iota(jnp.int32, (DA_TK, LANES), 1) == 0, 1.0, 0.0).astype(BF16)

    def scores(j, slot):
        kstart = pl.multiple_of(j * DA_TK, DA_TK)
        krel = (kstart - i * tq + lax.broadcasted_iota(jnp.int32, (1, DA_TK), 1)).astype(F32)
        for hh in range(DA_HEADS_PER_STEP):
            hs = slice(hh * LANES, (hh + 1) * LANES)
            q2 = jnp.concatenate([q0_ref[:, hs], q1_ref[:, hs]], axis=0)
            s_s[slot, hh] = (_nt_dot(q2, k_ref[pl.ds(kstart, DA_TK), hs])
                             + slope_ref[hp * DA_HEADS_PER_STEP + hh] * krel)

    def softmax_pv(j, slot, masked):
        kstart = pl.multiple_of(j * DA_TK, DA_TK)
        for hh in range(DA_HEADS_PER_STEP):
            hs = slice(hh * LANES, (hh + 1) * LANES)
            v_ext = jnp.concatenate([v_ref[pl.ds(kstart, DA_TK), hs], one_col], axis=1)
            s = s_s[slot, hh]
            if masked:
                row = lax.broadcasted_iota(jnp.int32, s.shape, 0)
                col = lax.broadcasted_iota(jnp.int32, s.shape, 1)
                rq = jnp.where(row >= tq, row - tq, row)
                s = jnp.where(col + (kstart - i * tq) <= rq, s, NEG_INF)
            m_prev = m_s[hh]
            m_new = jnp.maximum(m_prev, jnp.max(s, axis=-1, keepdims=True))
            alpha = jnp.exp(m_prev - m_new)
            p = jnp.exp(s - m_new).astype(BF16)
            acc_s[hh] = alpha * acc_s[hh] + jnp.dot(p, v_ext, preferred_element_type=F32)
            m_s[hh] = m_new

    last = (i * tq) // DA_TK

    def block(j, slot):
        @pl.when(j < last)
        def _():
            scores(j + 1, 1 - slot)
            softmax_pv(j, slot, False)

        @pl.when(j == last)
        def _():
            softmax_pv(j, slot, True)

    def pair(jj, carry):
        block(2 * jj, 0)
        block(2 * jj + 1, 1)
        return carry

    scores(0, 0)
    lax.fori_loop(0, last // 2 + 1, pair, 0)

    lamv = lamv_ref[...]
    lam = (jnp.exp(jnp.sum(lamv[0:1] * lamv[1:2], axis=-1, keepdims=True))
           - jnp.exp(jnp.sum(lamv[2:3] * lamv[3:4], axis=-1, keepdims=True)) + LAM_INIT)
    for hh in range(DA_HEADS_PER_STEP):
        acc = acc_s[hh]
        o = acc[:, 0:DA_V_DIM] / acc[:, DA_V_DIM:DA_V_DIM + 1]
        a = o[:tq] - lam * o[tq:]
        y = a * lax.rsqrt(jnp.mean(a * a, axis=-1, keepdims=True) + SUBLN_EPS) * g_ref[...]
        o_ref[:, hh * LANES:(hh + 1) * LANES] = (y * (1.0 - LAM_INIT)).astype(BF16)


def _diff_attention(q0, q1, k, v, slopes, lamv, subln_g):
    b, s, _ = q0.shape
    assert DA_TK % DA_TQ == 0 and s % DA_TK == 0
    hw = DA_HEADS_PER_STEP * LANES
    qspec = pl.BlockSpec((None, DA_TQ, hw), lambda bi, hi, qi: (bi, qi, hi))
    kvspec = pl.BlockSpec((None, s, hw), lambda bi, hi, qi: (bi, 0, hi))
    return pl.pallas_call(
        _da_kernel,
        grid=(b, DA_HEADS // DA_HEADS_PER_STEP, s // DA_TQ),
        in_specs=[pl.BlockSpec(memory_space=pltpu.SMEM),
                  pl.BlockSpec((4, DA_QK_DIM), lambda bi, hi, qi: (0, 0)),
                  pl.BlockSpec((1, DA_V_DIM), lambda bi, hi, qi: (0, 0)),
                  qspec, qspec, kvspec, kvspec],
        out_specs=qspec,
        out_shape=jax.ShapeDtypeStruct((b, s, DA_WIDTH), BF16),
        scratch_shapes=[pltpu.VMEM((DA_HEADS_PER_STEP, 2 * DA_TQ, 1), F32),
                        pltpu.VMEM((DA_HEADS_PER_STEP, 2 * DA_TQ, 2 * DA_V_DIM), F32),
                        pltpu.VMEM((2, DA_HEADS_PER_STEP, 2 * DA_TQ, DA_TK), F32)],
        compiler_params=pltpu.CompilerParams(dimension_semantics=("arbitrary",) * 3, vmem_limit_bytes=VMEM_LIMIT),
        name="diff_attention",
    )(slopes, lamv, subln_g, q0, q1, k, v)


def _mlstm_kernel(tril_ref, cw_ref, cb_ref, ng_ref, q_ref, k_ref, v_ref, og_ref, gate_ref, out_ref,
                  xbuf, cn_s, m_s):
    c = pl.program_id(1)
    L = q_ref.shape[0]

    @pl.when(c == 0)
    def _():
        xbuf[0:SUBLANES, :] = jnp.zeros((SUBLANES, 2 * ML_WIDTH), F32)
        cn_s[...] = jnp.zeros(cn_s.shape, F32)
        m_s[...] = jnp.full(m_s.shape, NEG_INF, F32)

    @pl.when(c > 0)
    def _():
        xbuf[0:SUBLANES, :] = xbuf[L:L + SUBLANES, :]

    xbuf[SUBLANES:SUBLANES + L, 0:ML_WIDTH] = q_ref[...]
    xbuf[SUBLANES:SUBLANES + L, ML_WIDTH:2 * ML_WIDTH] = k_ref[...]
    y = cb_ref[...] + cw_ref[0:1, :] * xbuf[SUBLANES - 3:SUBLANES - 3 + L, :]
    for j in range(1, ML_CONV):
        y = y + cw_ref[j:j + 1, :] * xbuf[SUBLANES - 3 + j:SUBLANES - 3 + j + L, :]
    qk = y * jax.nn.sigmoid(y)

    gates = gate_ref[...]
    i_log = gates[:, 0:LANES]
    f_log = jax.nn.log_sigmoid(gates[:, LANES:2 * LANES])
    b_all = jnp.dot(tril_ref[...], f_log, preferred_element_type=F32,
                    precision=lax.Precision.HIGHEST)
    r_all = i_log - b_all
    r_all_t = r_all.T
    row = lax.broadcasted_iota(jnp.int32, (L, L), 0)
    col = lax.broadcasted_iota(jnp.int32, (L, L), 1)
    tri = col <= row
    one_col = jnp.where(lax.broadcasted_iota(jnp.int32, (L, LANES), 1) == 0, 1.0, 0.0).astype(BF16)

    for h in range(ML_HEADS):
        hs = slice(h * ML_HEAD_DIM, (h + 1) * ML_HEAD_DIM)
        b_col = b_all[:, h:h + 1]
        r_col = r_all[:, h:h + 1]
        r_row = r_all_t[h:h + 1, :]
        m_prev = m_s[h:h + 1, 0:1]
        d = jnp.where(tri, b_col + r_row, NEG_INF)
        inter = b_col + m_prev
        mt = jnp.maximum(inter, jnp.max(d, axis=-1, keepdims=True))
        w_intra = jnp.exp(d - mt)
        w_inter = jnp.exp(inter - mt)
        qb = qk[:, hs].astype(BF16)
        kf = qk[:, ML_WIDTH + h * ML_HEAD_DIM:ML_WIDTH + (h + 1) * ML_HEAD_DIM] * (ML_HEAD_DIM ** -0.5)
        s = _nt_dot(qb, kf.astype(BF16)) * w_intra
        v_ext = jnp.concatenate([v_ref[:, hs], one_col], axis=1)
        cn = cn_s[h]
        num_ext = (w_inter * jnp.dot(qb, cn.astype(BF16), preferred_element_type=F32)
                   + jnp.dot(s.astype(BF16), v_ext, preferred_element_type=F32))
        num = num_ext[:, 0:ML_HEAD_DIM]
        den = num_ext[:, ML_HEAD_DIM:ML_HEAD_DIM + 1]
        hh = num / jnp.maximum(jnp.abs(den), jnp.exp(-mt))
        m_new = mt[L - 1:L, :]
        b_last = b_col[L - 1:L, :]
        ws = jnp.exp(b_last + r_col - m_new)
        decay = jnp.exp(b_last + m_prev - m_new)
        kw_t = (kf * ws).T.astype(BF16)
        cn_s[h] = decay * cn + jnp.dot(kw_t, v_ext, preferred_element_type=F32)
        m_s[h:h + 1, :] = jnp.broadcast_to(m_new, (1, LANES))
        hn = hh * lax.rsqrt(jnp.mean(hh * hh, axis=-1, keepdims=True) + NORM_EPS) * ng_ref[:, hs]
        out_ref[:, hs] = (hn * jax.nn.sigmoid(og_ref[:, hs])).astype(BF16)


def _mlstm(ml_q, ml_k, ml_v, ml_o, gates, conv_w, conv_b, norm_g):
    b, s, _ = ml_q.shape
    L = min(ML_CHUNK, s)
    tril = jnp.tril(jnp.ones((L, L), F32))
    fixed = lambda bi, ci: (0, 0)
    seq = lambda w: pl.BlockSpec((None, L, w), lambda bi, ci: (bi, ci, 0))
    return pl.pallas_call(
        _mlstm_kernel,
        grid=(b, s // L),
        in_specs=[pl.BlockSpec((L, L), fixed),
                  pl.BlockSpec((ML_CONV, 2 * ML_WIDTH), fixed),
                  pl.BlockSpec((1, 2 * ML_WIDTH), fixed),
                  pl.BlockSpec((1, ML_WIDTH), fixed),
                  seq(ML_WIDTH), seq(ML_WIDTH), seq(ML_WIDTH), seq(ML_WIDTH), seq(2 * LANES)],
        out_specs=seq(ML_WIDTH),
        out_shape=jax.ShapeDtypeStruct((b, s, ML_WIDTH), BF16),
        scratch_shapes=[pltpu.VMEM((L + 2 * SUBLANES, 2 * ML_WIDTH), F32),
                        pltpu.VMEM((ML_HEADS, ML_HEAD_DIM, 2 * ML_HEAD_DIM), F32),
                        pltpu.VMEM((SUBLANES, LANES), F32)],
        compiler_params=pltpu.CompilerParams(dimension_semantics=("arbitrary",) * 2, vmem_limit_bytes=VMEM_LIMIT),
        name="mlstm",
    )(tril, conv_w, conv_b, norm_g, ml_q, ml_k, ml_v, ml_o, gates)


def _extract_top_exact(s, n, out_ref, lead):
    rows = s.shape[0]
    iota = lax.broadcasted_iota(jnp.int32, s.shape, 0)
    for k in range(n):
        mx = jnp.max(s, axis=0, keepdims=True)
        out_ref[lead, k:k + 1, :] = mx
        first = jnp.min(jnp.where(s == mx, iota, rows), axis=0, keepdims=True)
        s = jnp.where(iota == first, -jnp.inf, s)
    return s


def _extract_top_fast(s, n, out_ref, lead, n_pad=0):
    for k in range(n):
        mx = jnp.max(s, axis=0, keepdims=True)
        out_ref[lead, k:k + 1, :] = mx
        s = jnp.where(s == mx, -jnp.inf, s)
    removed = jnp.sum(jnp.where(s == -jnp.inf, 1.0, 0.0), axis=0, keepdims=True)
    return s, removed - float(n + n_pad)


def _mid_kernel(a_ref, hm_ref, x_ref, wout_ref, g2_ref, wpq_ref, sk_ref,
                x1_ref, xnt_ref, need_ref, e1_ref, s2_ref, e2_ref, q_s, vals_s, cand_s):
    x1 = (x_ref[...]
          + jnp.dot(a_ref[...], wout_ref[0:DA_WIDTH, :], preferred_element_type=F32)
          + jnp.dot(hm_ref[...], wout_ref[DA_WIDTH:DA_WIDTH + ML_WIDTH, :], preferred_element_type=F32))
    x1_ref[...] = x1
    xn = x1 * lax.rsqrt(jnp.mean(x1 * x1, axis=-1, keepdims=True) + NORM_EPS) * g2_ref[...]
    xnt = xn.T.astype(BF16)
    tiles = [slice(i * PEER_UNIT_TOKENS, (i + 1) * PEER_UNIT_TOKENS) for i in range(xnt.shape[1] // PEER_UNIT_TOKENS)]
    for i, ts in enumerate(tiles):
        xnt_ref[i] = xnt[:, ts]
    q = jnp.dot(xn.astype(BF16), wpq_ref[...], preferred_element_type=F32).astype(BF16)
    for hc in range(2 * PEER_HEADS):
        q_s[hc] = q[:, hc * PEER_HALF:(hc + 1) * PEER_HALF]
    tt = x1.shape[0]
    cand_s[N_CAND:N_CAND_PAD, :] = jnp.full((N_CAND_PAD - N_CAND, tt), -jnp.inf, F32)

    def head(h, carry):
        s1 = _nt_dot(sk_ref[2 * h], q_s[2 * h])
        s2 = _nt_dot(sk_ref[2 * h + 1], q_s[2 * h + 1])

        def select(extract):
            r1 = extract(s1, 0)
            r2 = extract(s2, 1)
            for i, cnt in enumerate(CAND_COUNTS):
                off = CAND_OFFSETS[i]
                cand_s[off:off + cnt, :] = vals_s[0, i:i + 1, :] + vals_s[1, 0:cnt, :]
            r3 = extract(cand_s[...], 2, N_CAND_PAD - N_CAND)
            cand_s[...] = r3[0]
            return jnp.maximum(jnp.maximum(r1[1], r2[1]), r3[1])

        def fast(s, lead, n_pad=0):
            return _extract_top_fast(s, PEER_TOPK, vals_s, lead, n_pad)

        def exact(s, lead, n_pad=0):
            return _extract_top_exact(s, PEER_TOPK, vals_s, lead), jnp.zeros((1, tt), F32)

        excess = select(fast)

        @pl.when(jnp.max(excess) > 0.0)
        def _():
            select(exact)

        top = vals_s[2, 0:PEER_TOPK, :]
        t0 = top[0:1, :]
        lse = t0 + jnp.log(jnp.sum(jnp.exp(top - t0), axis=0, keepdims=True))
        need = jnp.full(s1.shape, jnp.inf, F32)
        for i in reversed(range(PEER_TOPK)):
            cnt, off = CAND_COUNTS[i], CAND_OFFSETS[i]
            cut = jnp.min(jnp.where(cand_s[off:off + cnt, :] == -jnp.inf, vals_s[1, 0:cnt, :], jnp.inf),
                          axis=0, keepdims=True)
            need = jnp.where(s1 >= vals_s[0, i:i + 1, :], cut, need)
        c1 = vals_s[0, 0:1, :]
        e1 = jnp.exp(s1 - c1)
        e2 = 0.5 * jnp.exp(s2 - (lse - c1))
        for i, ts in enumerate(tiles):
            need_ref[i, h] = need[:, ts]
            e1_ref[i, h] = e1[:, ts]
            s2_ref[i, h] = s2[:, ts]
            e2_ref[i, h] = e2[:, ts]
        return carry

    lax.fori_loop(0, PEER_HEADS, head, 0)


def _mid(a, hm, x2, w_out, norm2_g, w_pq, sub_keys):
    t = x2.shape[0]
    tt = min(MID_TOKENS, t)
    tile = PEER_UNIT_TOKENS
    n_tile = tt // tile
    row = lambda i: (i, 0)
    fixed = lambda i: (0, 0)
    return pl.pallas_call(
        _mid_kernel,
        grid=(t // tt,),
        in_specs=[pl.BlockSpec((tt, DA_WIDTH), row),
                  pl.BlockSpec((tt, ML_WIDTH), row),
                  pl.BlockSpec((tt, D_MODEL), row),
                  pl.BlockSpec((D_MODEL, D_MODEL), fixed),
                  pl.BlockSpec((1, D_MODEL), fixed),
                  pl.BlockSpec((D_MODEL, 2 * PEER_HEADS * PEER_HALF), fixed),
                  pl.BlockSpec((2 * PEER_HEADS, PEER_NKEYS, PEER_HALF), lambda i: (0, 0, 0))],
        out_specs=[pl.BlockSpec((tt, D_MODEL), row),
                   pl.BlockSpec((n_tile, D_MODEL, tile), lambda i: (i, 0, 0))]
                  + [pl.BlockSpec((n_tile, PEER_HEADS, PEER_NKEYS, tile), lambda i: (i, 0, 0, 0))] * 4,
        out_shape=[jax.ShapeDtypeStruct((t, D_MODEL), F32),
                   jax.ShapeDtypeStruct((t // tile, D_MODEL, tile), BF16)]
                  + [jax.ShapeDtypeStruct((t // tile, PEER_HEADS, PEER_NKEYS, tile), F32)] * 4,
        scratch_shapes=[pltpu.VMEM((2 * PEER_HEADS, tt, PEER_HALF), BF16),
                        pltpu.VMEM((3, 2 * SUBLANES, tt), F32),
                        pltpu.VMEM((N_CAND_PAD, tt), F32)],
        compiler_params=pltpu.CompilerParams(dimension_semantics=("arbitrary",), vmem_limit_bytes=VMEM_LIMIT),
        name="mid",
    )(a, hm, x2, w_out, norm2_g, w_pq, sub_keys)


def _gate_unit(key0, act_ref, need_ref, e1_ref, s2_ref, e2_ref, g_ref, row0, th):
    for q in range(PEER_PIECE // PEER_NKEYS):
        need_rows = [need_ref[th, h, pl.ds(key0 + q, 1), :] for h in range(PEER_HEADS)]
        e1_rows = [e1_ref[th, h, pl.ds(key0 + q, 1), :] for h in range(PEER_HEADS)]
        for lt in range(PEER_UNIT_TOKENS // LANES):
            la = slice(lt * LANES, (lt + 1) * LANES)
            ls = slice(th * PEER_UNIT_TOKENS + lt * LANES, th * PEER_UNIT_TOKENS + (lt + 1) * LANES)
            w = None
            for h in range(PEER_HEADS):
                wh = jnp.where(s2_ref[th, h, :, la] >= need_rows[h][:, la], e2_ref[th, h, :, la], 0.0) * e1_rows[h][:, la]
                w = wh if w is None else w + wh
            a = act_ref[q * PEER_NKEYS:(q + 1) * PEER_NKEYS, la]
            g_ref[pl.ds(pl.multiple_of(row0 + q * PEER_NKEYS, PEER_NKEYS), PEER_NKEYS), ls] = (
                a * (1.0 + lax.erf(a * np.float32(np.sqrt(0.5)))) * w).astype(BF16)


def _peer_kernel(xnt_ref, u_ref, u_next_ref, vt_a_ref, vt_bprev_ref, vt_last_ref, need_ref, e1_ref, s2_ref, e2_ref,
                 x1_ref, fg_ref, y_ref, ga_s, gb_s, acc_s, act_s):
    k = pl.program_id(1)
    per_chunk = PEER_ECHUNK // PEER_PIECE
    n_piece = 2 * per_chunk
    keys_per_piece = PEER_PIECE // PEER_NKEYS

    def act_unit(u_rows, th):
        act_s[th] = jnp.dot(u_rows, xnt_ref[th], preferred_element_type=F32)

    def u_piece(p):
        return u_ref[pl.ds(pl.multiple_of(p * PEER_PIECE, PEER_PIECE), PEER_PIECE), :]

    @pl.when(k == 0)
    def _():
        acc_s[...] = jnp.zeros(acc_s.shape, F32)
        gb_s[...] = jnp.zeros(gb_s.shape, BF16)
        act_unit(u_piece(0), 0)

    def piece(r, c, last):
        g_prev, g_cur = (gb_s, ga_s) if c == 0 else (ga_s, gb_s)
        p = c * per_chunk + r
        rows = pl.ds(pl.multiple_of(r * PEER_PIECE, PEER_PIECE), PEER_PIECE)
        v_rows = vt_bprev_ref[rows, :] if c == 0 else vt_a_ref[rows, :]
        for th in range(2):
            ts = slice(th * PEER_UNIT_TOKENS, (th + 1) * PEER_UNIT_TOKENS)
            if th == 0:
                act_unit(u_piece(p), 1)
            else:
                act_unit(u_next_ref[...] if last else u_piece(p + 1), 0)
            acc_s[rows, ts] += jnp.dot(v_rows, g_prev[:, ts], preferred_element_type=F32)
            _gate_unit(p * keys_per_piece, act_s.at[th], need_ref, e1_ref, s2_ref, e2_ref, g_cur,
                       r * PEER_PIECE, th)

    def rolled(c, first, n, last):
        def body(r, carry):
            piece(r, c, last)
            return carry
        lax.fori_loop(first, jnp.minimum(k + first + n, first + n), body, 0)

    rolled(0, 0, per_chunk, False)
    rolled(1, 0, per_chunk - 1, False)
    rolled(1, per_chunk - 1, 1, True)

    @pl.when(k == pl.num_programs(1) - 1)
    def _():
        acc = acc_s[...] + jnp.dot(vt_last_ref[...], gb_s[...], preferred_element_type=F32)
        x2 = x1_ref[...] + acc.T
        y_ref[...] = x2 * lax.rsqrt(jnp.mean(x2 * x2, axis=-1, keepdims=True) + NORM_EPS) * fg_ref[...]


def _peer(xnt, u_bf, vt_bf, need, e1, s2, e2, x1, final_g):
    t = x1.shape[0]
    tt = min(PEER_TOKENS, t)
    assert tt == 2 * PEER_UNIT_TOKENS and need.shape[-1] == PEER_UNIT_TOKENS
    step = 2 * PEER_ECHUNK
    n_step = PEER_EXPERTS // step
    step_keys = step // PEER_NKEYS
    halves = tt // PEER_UNIT_TOKENS
    key_rows = pl.BlockSpec((halves, PEER_HEADS, step_keys, PEER_UNIT_TOKENS), lambda i, e: (i, 0, e, 0))
    all_keys = pl.BlockSpec((halves, PEER_HEADS, PEER_NKEYS, PEER_UNIT_TOKENS), lambda i, e: (i, 0, 0, 0))
    return pl.pallas_call(
        _peer_kernel,
        grid=(t // tt, n_step),
        in_specs=[pl.BlockSpec((halves, D_MODEL, PEER_UNIT_TOKENS), lambda i, e: (i, 0, 0)),
                  pl.BlockSpec((step, D_MODEL), lambda i, e: (e, 0)),
                  pl.BlockSpec((PEER_PIECE, D_MODEL),
                               lambda i, e: (jnp.minimum(e + 1, n_step - 1) * (step // PEER_PIECE), 0)),
                  pl.BlockSpec((D_MODEL, PEER_ECHUNK), lambda i, e: (0, 2 * e)),
                  pl.BlockSpec((D_MODEL, PEER_ECHUNK), lambda i, e: (0, jnp.maximum(2 * e - 1, 0))),
                  pl.BlockSpec((D_MODEL, PEER_ECHUNK), lambda i, e: (0, 2 * n_step - 1)),
                  key_rows, key_rows, all_keys, all_keys,
                  pl.BlockSpec((tt, D_MODEL), lambda i, e: (i, 0)),
                  pl.BlockSpec((1, D_MODEL), lambda i, e: (0, 0))],
        out_specs=pl.BlockSpec((tt, D_MODEL), lambda i, e: (i, 0)),
        out_shape=jax.ShapeDtypeStruct((t, D_MODEL), F32),
        scratch_shapes=[pltpu.VMEM((PEER_ECHUNK, tt), BF16),
                        pltpu.VMEM((PEER_ECHUNK, tt), BF16),
                        pltpu.VMEM((D_MODEL, tt), F32),
                        pltpu.VMEM((2, PEER_PIECE, PEER_UNIT_TOKENS), F32)],
        compiler_params=pltpu.CompilerParams(dimension_semantics=("arbitrary",) * 2, vmem_limit_bytes=VMEM_LIMIT),
        name="peer_experts",
    )(xnt, u_bf, u_bf, vt_bf, vt_bf, vt_bf, need, e1, s2, e2, x1, final_g)


def _alibi_slopes(n):
    start = 2.0 ** (-8.0 / n)
    return jnp.asarray(np.array([start ** (h + 1) for h in range(n)], dtype=np.float32))


def kernel(x, norm1_g, w_in, conv_w, conv_b, b_igate, b_fgate, lam_q1, lam_k1, lam_q2, lam_k2, da_subln_g, ml_norm_g, w_out, norm2_g, w_pq, sub_keys, u_emb, v_emb, final_g):
    bsz, seq, d = x.shape
    t = bsz * seq
    x2 = x.reshape(t, d)
    l = 0
    w_main = w_in[l][:, :N_MAIN].astype(BF16)
    gate_pad = jnp.zeros((d, LANES - ML_HEADS), F32)
    w_gate = jnp.concatenate([w_in[l][:, N_MAIN:N_MAIN + ML_HEADS], gate_pad,
                              w_in[l][:, N_MAIN + ML_HEADS:], gate_pad], axis=1).astype(BF16)
    bias_pad = jnp.zeros((LANES - ML_HEADS,), F32)
    b_gate = jnp.concatenate([b_igate[l], bias_pad, b_fgate[l], bias_pad])[None, :]
    lamv = jnp.stack([lam_q1[l], lam_k1[l], lam_q2[l], lam_k2[l]])

    q0, q1, da_k, da_v, ml_q, ml_k, ml_v, ml_o, gates = _inproj(x2, norm1_g[l][None, :], w_main, w_gate, b_gate)
    seq3 = lambda z: z.reshape(bsz, seq, z.shape[-1])
    a = _diff_attention(seq3(q0), seq3(q1), seq3(da_k), seq3(da_v), _alibi_slopes(DA_HEADS), lamv,
                        da_subln_g[l][None, :])
    hm = _mlstm(seq3(ml_q), seq3(ml_k), seq3(ml_v), seq3(ml_o), seq3(gates),
                conv_w[l], conv_b[l][None, :], ml_norm_g[l][None, :])
    x1, xnt, need, e1, s2, e2 = _mid(a.reshape(t, DA_WIDTH), hm.reshape(t, ML_WIDTH), x2, w_out[l].astype(BF16),
                                     norm2_g[l][None, :], w_pq[l].astype(BF16),
                                     sub_keys[l].reshape(2 * PEER_HEADS, PEER_NKEYS, PEER_HALF).astype(BF16))
    y = _peer(xnt, u_emb[l].astype(BF16), v_emb[l].T.astype(BF16), need, e1, s2, e2, x1, final_g[None, :])
    return y.reshape(bsz, seq, d)
```
